```python
import jax, jax.numpy as jnp
from jax import lax
import numpy as np

D_MODEL = 1024
BATCH = 32
SEQ = 256
DEPTH = 1
DEC_BATCH = 8
DEC_SEQ = 2048
PAST_LEN = 256

GRID_W = 64
HEAD_DIM = 64
N_HEADS_A = 8
N_KV_A = 2
N_HEADS_B = 8
N_KV_B = 2
WIDTH_A = N_HEADS_A * HEAD_DIM
WIDTH_B = N_HEADS_B * HEAD_DIM
MIX_WIDTH = WIDTH_A + WIDTH_B
KV_W_A = N_KV_A * HEAD_DIM
KV_W_B = N_KV_B * HEAD_DIM
SPLIT_SIZES = (WIDTH_A, KV_W_A, KV_W_A, WIDTH_A, WIDTH_B, KV_W_B, KV_W_B, WIDTH_B)
SPLIT_IDX = tuple(int(v) for v in np.cumsum(SPLIT_SIZES)[:-1])
IN_WIDTH = int(sum(SPLIT_SIZES))
Q_BLOCK = 128
WINDOW = 128
ROPE_THETA = 10000.0
EPS = 1e-6
NEG_INF = -1e30

kernel_name = "hybrid_parallel_gqa_window_diffusion_step"


def rmsnorm(x, gain):
    xf = x.astype(jnp.float32)
    y = xf * lax.rsqrt(jnp.mean(xf * xf, axis=-1, keepdims=True) + EPS)
    return (y * gain.astype(jnp.float32)).astype(x.dtype)


def modulation(cond, w_mod, b_mod):
    m = (jax.nn.silu(cond) @ w_mod + b_mod)[..., None, :]
    shift, scale, gate = jnp.split(m, 3, axis=-1)
    return shift, scale, gate


def axial_rope(n_tokens):
    rows = n_tokens // GRID_W
    row = jnp.repeat(jnp.arange(rows, dtype=jnp.float32), GRID_W)
    col = jnp.tile(jnp.arange(GRID_W, dtype=jnp.float32), rows)
    n_freq = HEAD_DIM // 4
    inv = ROPE_THETA ** (-jnp.arange(n_freq, dtype=jnp.float32) / n_freq)
    ar = row[:, None] * inv[None, :]
    ac = col[:, None] * inv[None, :]
    ang = jnp.concatenate([ar, ar, ac, ac], axis=-1)
    return jnp.cos(ang)[:, None, :], jnp.sin(ang)[:, None, :]


def apply_rope(x, cos, sin):
    xf = x.astype(jnp.float32)
    r1, r2, c1, c2 = jnp.split(xf, 4, axis=-1)
    rot = jnp.concatenate([-r2, r1, -c2, c1], axis=-1)
    return (xf * cos + rot * sin).astype(x.dtype)


def project(h, w_in, qn_a, kn_a, qn_b, kn_b):
    b, t, _ = h.shape
    p = h @ w_in
    q_a, k_a, v_a, g_a, q_b, k_b, v_b, g_b = jnp.split(p, SPLIT_IDX, axis=-1)
    q_a = rmsnorm(q_a.reshape(b, t, N_HEADS_A, HEAD_DIM), qn_a)
    k_a = rmsnorm(k_a.reshape(b, t, N_KV_A, HEAD_DIM), kn_a)
    v_a = v_a.reshape(b, t, N_KV_A, HEAD_DIM)
    q_b = rmsnorm(q_b.reshape(b, t, N_HEADS_B, HEAD_DIM), qn_b)
    k_b = rmsnorm(k_b.reshape(b, t, N_KV_B, HEAD_DIM), kn_b)
    v_b = v_b.reshape(b, t, N_KV_B, HEAD_DIM)
    return q_a, k_a, v_a, g_a, q_b, k_b, v_b, g_b


def sink_column(sink, kv, shape_prefix):
    s = sink.astype(jnp.float32).reshape(kv, -1)[:, :, None, None]
    return jnp.broadcast_to(s, shape_prefix + (1,))


def gqa_dense(q, k, v, sink=None):
    b, t, h, d = q.shape
    kv = k.shape[2]
    qg = q.reshape(b, t, kv, h // kv, d)
    s = jnp.einsum('btkgd,bskd->bkgts', qg, k).astype(jnp.float32) * (d ** -0.5)
    if sink is not None:
        s = jnp.concatenate([sink_column(sink, kv, s.shape[:-1]), s], axis=-1)
        p = jax.nn.softmax(s, axis=-1)[..., 1:]
    else:
        p = jax.nn.softmax(s, axis=-1)
    o = jnp.einsum('bkgts,bskd->btkgd', p.astype(v.dtype), v)
    return o.reshape(b, t, h * d)


def global_attn_latent(q, k_lat, v_lat, k_ctx, v_ctx):
    b, n, h, d = q.shape
    kv = k_lat.shape[2]
    g = h // kv
    k_all = jnp.concatenate([k_ctx, k_lat], axis=1)
    v_all = jnp.concatenate([v_ctx, v_lat], axis=1)
    nb = n // Q_BLOCK
    qb = q.reshape(b, nb, Q_BLOCK, kv, g, d).transpose(1, 0, 2, 3, 4, 5)

    def one_block(qblk):
        s = jnp.einsum('bqkgd,bskd->bkgqs', qblk, k_all).astype(jnp.float32) * (d ** -0.5)
        p = jax.nn.softmax(s, axis=-1).astype(v_all.dtype)
        return jnp.einsum('bkgqs,bskd->bqkgd', p, v_all)

    o = lax.map(one_block, qb)
    return o.transpose(1, 0, 2, 3, 4, 5).reshape(b, n, h * d)


def window_attn_latent(q, k_lat, v_lat, k_ctx, v_ctx, sink):
    b, n, h, d = q.shape
    kv = k_lat.shape[2]
    g = h // kv
    nb = n // Q_BLOCK
    pad = ((0, 0), (Q_BLOCK, Q_BLOCK), (0, 0), (0, 0))
    kp = jnp.pad(k_lat, pad).reshape(b, nb + 2, Q_BLOCK, kv, d)
    vp = jnp.pad(v_lat, pad).reshape(b, nb + 2, Q_BLOCK, kv, d)
    kband = jnp.concatenate([kp[:, :-2], kp[:, 1:-1], kp[:, 2:]], axis=2)
    vband = jnp.concatenate([vp[:, :-2], vp[:, 1:-1], vp[:, 2:]], axis=2)
    qb = q.reshape(b, nb, Q_BLOCK, kv, g, d)
    scale = d ** -0.5
    s_band = jnp.einsum('bnqkgd,bnskd->bnkgqs', qb, kband).astype(jnp.float32) * scale
    qpos = jnp.arange(Q_BLOCK)[:, None]
    kpos = jnp.arange(3 * Q_BLOCK)[None, :] - Q_BLOCK
    rel = kpos - qpos
    abs_j = (jnp.arange(nb) * Q_BLOCK)[:, None, None] + kpos[None]
    valid = (jnp.abs(rel)[None] <= WINDOW) & (abs_j >= 0) & (abs_j < n)
    s_band = jnp.where(valid[None, :, None, None], s_band, NEG_INF)
    s_ctx = jnp.einsum('bnqkgd,bskd->bnkgqs', qb, k_ctx).astype(jnp.float32) * scale
    n_ctx = k_ctx.shape[1]
    s = jnp.concatenate([sink_column(sink, kv, s_ctx.shape[:-1]), s_ctx, s_band], axis=-1)
    p = jax.nn.softmax(s, axis=-1).astype(v_lat.dtype)
    p_ctx = p[..., 1:1 + n_ctx]
    p_band = p[..., 1 + n_ctx:]
    o = (jnp.einsum('bnkgqs,bskd->bnqkgd', p_ctx, v_ctx)
         + jnp.einsum('bnkgqs,bnskd->bnqkgd', p_band, vband))
    return o.reshape(b, n, h * d)


def merge_out(o_a, g_a, o_b, g_b, w_out):
    return jnp.concatenate([o_a * jax.nn.silu(g_a), o_b * jax.nn.silu(g_b)], axis=-1) @ w_out


def setup_inputs(seed: int = 0) -> dict:
    key = jax.random.key(seed)
    ks = jax.random.split(key, 20)
    f32 = jnp.float32
    nrm = lambda k, shape, s: jax.random.normal(k, shape, f32) * s
    return {
        "x_prompt": nrm(ks[0], (BATCH, SEQ, D_MODEL), 1.0),
        "x_sample": nrm(ks[1], (DEC_BATCH, DEC_SEQ, D_MODEL), 1.0),
        "cache_k_a": nrm(ks[2], (DEC_BATCH, DEPTH, PAST_LEN, N_KV_A, HEAD_DIM), 1.0),
        "cache_v_a": nrm(ks[3], (DEC_BATCH, DEPTH, PAST_LEN, N_KV_A, HEAD_DIM), 1.0),
        "cache_k_b": nrm(ks[4], (DEC_BATCH, DEPTH, PAST_LEN, N_KV_B, HEAD_DIM), 1.0),
        "cache_v_b": nrm(ks[5], (DEC_BATCH, DEPTH, PAST_LEN, N_KV_B, HEAD_DIM), 1.0),
        "c": nrm(ks[6], (DEC_BATCH, D_MODEL), 1.0),
        "c_ctx": nrm(ks[7], (D_MODEL,), 1.0),
        "w_mod": nrm(ks[8], (DEPTH, D_MODEL, 3 * D_MODEL), 0.02),
        "b_mod": nrm(ks[9], (DEPTH, 3 * D_MODEL), 0.02),
        "norm_gain": 1.0 + nrm(ks[10], (DEPTH, D_MODEL), 0.02),
        "w_in": nrm(ks[11], (DEPTH, D_MODEL, IN_WIDTH), D_MODEL ** -0.5),
        "qn_a": 1.0 + nrm(ks[12], (DEPTH, HEAD_DIM), 0.02),
        "kn_a": 1.0 + nrm(ks[13], (DEPTH, HEAD_DIM), 0.02),
        "qn_b": 1.0 + nrm(ks[14], (DEPTH, HEAD_DIM), 0.02),
        "kn_b": 1.0 + nrm(ks[15], (DEPTH, HEAD_DIM), 0.02),
        "sink_b": nrm(ks[16], (DEPTH, N_HEADS_B), 0.5),
        "w_out": nrm(ks[17], (DEPTH, MIX_WIDTH, D_MODEL), MIX_WIDTH ** -0.5),
    }


def reference(x_prompt, x_sample, cache_k_a, cache_v_a, cache_k_b, cache_v_b, c, c_ctx,
              w_mod, b_mod, norm_gain, w_in, qn_a, kn_a, qn_b, kn_b, sink_b, w_out):
    n_lat = x_sample.shape[1]
    cos, sin = axial_rope(n_lat)
    xp = x_prompt
    xs = x_sample
    new_k_a, new_v_a, new_k_b, new_v_b = [], [], [], []
    for l in range(DEPTH):
        shift, scale, gate = modulation(c_ctx, w_mod[l], b_mod[l])
        h = rmsnorm(xp, norm_gain[l]) * (1.0 + scale) + shift
        q_a, k_a, v_a, g_a, q_b, k_b, v_b, g_b = project(h, w_in[l], qn_a[l], kn_a[l], qn_b[l], kn_b[l])
        o_a = gqa_dense(q_a, k_a, v_a)
        o_b = gqa_dense(q_b, k_b, v_b, sink_b[l])
        xp = xp + gate * merge_out(o_a, g_a, o_b, g_b, w_out[l])
        new_k_a.append(k_a)
        new_v_a.append(v_a)
        new_k_b.append(k_b)
        new_v_b.append(v_b)

        shift, scale, gate = modulation(c, w_mod[l], b_mod[l])
        h = rmsnorm(xs, norm_gain[l]) * (1.0 + scale) + shift
        q_a, k_a, v_a, g_a, q_b, k_b, v_b, g_b = project(h, w_in[l], qn_a[l], kn_a[l], qn_b[l], kn_b[l])
        q_a = apply_rope(q_a, cos, sin)
        k_a = apply_rope(k_a, cos, sin)
        q_b = apply_rope(q_b, cos, sin)
        k_b = apply_rope(k_b, cos, sin)
        o_a = global_attn_latent(q_a, k_a, v_a, cache_k_a[:, l], cache_v_a[:, l])
        o_b = window_attn_latent(q_b, k_b, v_b, cache_k_b[:, l], cache_v_b[:, l], sink_b[l])
        xs = xs + gate * merge_out(o_a, g_a, o_b, g_b, w_out[l])
    nk_a = jnp.stack(new_k_a, axis=1)
    nv_a = jnp.stack(new_v_a, axis=1)
    nk_b = jnp.stack(new_k_b, axis=1)
    nv_b = jnp.stack(new_v_b, axis=1)
    return (xp, xs, nk_a, nv_a, nk_b, nv_b)
```

```python
import functools

import numpy as np
import jax
import jax.numpy as jnp
from jax import lax
from jax.experimental import pallas as pl
from jax.experimental.pallas import tpu as pltpu

F32 = jnp.float32
BF16 = jnp.bfloat16

D_MODEL = 1024
HEAD_DIM = 64
N_HEADS_A = 8
N_KV_A = 2
N_HEADS_B = 8
N_KV_B = 2
N_HEADS = N_HEADS_A + N_HEADS_B
GROUP = N_HEADS_A // N_KV_A
WIDTH_A = N_HEADS_A * HEAD_DIM
WIDTH_B = N_HEADS_B * HEAD_DIM
MIX_WIDTH = WIDTH_A + WIDTH_B
KV_W = N_KV_A * HEAD_DIM
IN_WIDTH = 2 * (2 * WIDTH_A + 2 * KV_W)
GRID_W = 64
WINDOW = 128
ROPE_THETA = 10000.0
EPS = 1e-6
NEG_INF = -1e30

LANES = 128
MXU_DIM = 256
VMEM_LIMIT_BYTES = 56 * 1024 * 1024

_OFF_QA = 0
_OFF_KA = _OFF_QA + WIDTH_A
_OFF_VA = _OFF_KA + KV_W
_OFF_GA = _OFF_VA + KV_W
_OFF_QB = _OFF_GA + WIDTH_A
_OFF_KB = _OFF_QB + WIDTH_B
_OFF_VB = _OFF_KB + KV_W
_OFF_GB = _OFF_VB + KV_W

TOKEN_TILE = 256


def _dot(a, b):
    return jnp.dot(a, b, preferred_element_type=F32)


def _dot_t(a, b):
    return lax.dot_general(a, b, (((1,), (1,)), ((), ())), preferred_element_type=F32)


def _mod_kernel(cond_ref, w_ref, b_ref, out_ref):
    c = cond_ref[...]
    s = c * jax.nn.sigmoid(c)
    out_ref[...] = _dot(s.astype(BF16), w_ref[...].astype(BF16)) + b_ref[...]


def _modulation(cond, w_mod, b_mod):
    rows = cond.shape[0]
    n_out = w_mod.shape[1]
    bn = 768
    return pl.pallas_call(
        _mod_kernel,
        grid=(n_out // bn,),
        in_specs=[
            pl.BlockSpec((rows, D_MODEL), lambda j: (0, 0)),
            pl.BlockSpec((D_MODEL, bn), lambda j: (0, j)),
            pl.BlockSpec((1, bn), lambda j: (0, j)),
        ],
        out_specs=pl.BlockSpec((rows, bn), lambda j: (0, j)),
        out_shape=jax.ShapeDtypeStruct((rows, n_out), F32),
        compiler_params=pltpu.CompilerParams(
            dimension_semantics=("arbitrary",), vmem_limit_bytes=VMEM_LIMIT_BYTES),
        name="modulation",
    )(cond, w_mod, b_mod.reshape(1, n_out))


def _head_rms(blk, ones_blockdiag):
    ss = _dot((blk * blk).astype(BF16), ones_blockdiag)
    return blk * lax.rsqrt(ss * (1.0 / HEAD_DIM) + EPS)


def _rope(blk, cos, sin_signed, low_half):
    up = pltpu.roll(blk, LANES - 16, 1)
    down = pltpu.roll(blk, 16, 1)
    return blk * cos + jnp.where(low_half, up, down) * sin_signed


def _proj_kernel(*refs, rope, emit_f32):
    it = iter(refs)
    x_ref, shift_ref, scale_ref, gain_ref, w_ref, hg_ref, ones_ref = (next(it) for _ in range(7))
    if rope:
        cos_ref, sin_ref = next(it), next(it)
    q_ref, g_ref, ka_ref, va_ref, kb_ref, vb_ref = (next(it) for _ in range(6))
    if emit_f32:
        ka32_ref, va32_ref, kb32_ref, vb32_ref = (next(it) for _ in range(4))

    x = x_ref[0]
    ms = jnp.mean(x * x, axis=-1, keepdims=True)
    h = x * lax.rsqrt(ms + EPS) * gain_ref[...]
    h = h * (1.0 + scale_ref[0]) + shift_ref[0]
    hb = h.astype(BF16)

    lane = lax.broadcasted_iota(jnp.int32, (1, LANES), 1)
    if rope:
        cos = cos_ref[...]
        sin_signed = sin_ref[...]
        low_half = (lane % 32) < 16

    ones256 = ones_ref[...]
    ones128 = ones_ref[0:LANES, 0:LANES]

    def seg(off, width):
        return _dot(hb, w_ref[:, off:off + width])

    def normed_chunks(off, width, gain):
        p = seg(off, width)
        out = []
        step = MXU_DIM if width >= MXU_DIM else LANES
        for c0 in range(0, width, step):
            y = _head_rms(p[:, c0:c0 + step], ones256 if step == MXU_DIM else ones128)
            for c1 in range(0, step, LANES):
                yc = y[:, c1:c1 + LANES] * gain
                if rope:
                    yc = _rope(yc, cos, sin_signed, low_half)
                out.append(yc)
        return out

    def store_q(chunks, head0, gain_unused=None):
        for hh in range(2 * len(chunks)):
            kv = hh // GROUP
            c = chunks[hh // 2]
            if hh % 2 != kv:
                c = pltpu.roll(c, HEAD_DIM, 1)
            keep = (lane < HEAD_DIM) if kv == 0 else (lane >= HEAD_DIM)
            q_ref[0, head0 + hh] = jnp.where(keep, c, 0.0).astype(BF16)

    def silu(v):
        return v * jax.nn.sigmoid(v)

    store_q(normed_chunks(_OFF_QA, WIDTH_A, hg_ref[0:1, :]), 0)
    ka = normed_chunks(_OFF_KA, KV_W, hg_ref[1:2, :])[0]
    va = seg(_OFF_VA, KV_W)
    ka_ref[0] = ka.astype(BF16)
    va_ref[0] = va.astype(BF16)
    g_ref[0, :, 0:WIDTH_A] = silu(seg(_OFF_GA, WIDTH_A)).astype(BF16)
    store_q(normed_chunks(_OFF_QB, WIDTH_B, hg_ref[2:3, :]), N_HEADS_A)
    kb = normed_chunks(_OFF_KB, KV_W, hg_ref[3:4, :])[0]
    vb = seg(_OFF_VB, KV_W)
    kb_ref[0] = kb.astype(BF16)
    vb_ref[0] = vb.astype(BF16)
    g_ref[0, :, WIDTH_A:MIX_WIDTH] = silu(seg(_OFF_GB, WIDTH_B)).astype(BF16)
    if emit_f32:
        ka32_ref[0] = ka
        va32_ref[0] = va
        kb32_ref[0] = kb
        vb32_ref[0] = vb


def _project(x, shift, scale, gain, w_in_bf, head_gains, ones_bd, rope_tables, emit_f32):
    b, n, _ = x.shape
    tm = TOKEN_TILE
    rope = rope_tables is not None
    per_batch = shift.shape[0] != 1
    mod_map = (lambda i, j: (i, 0, 0)) if per_batch else (lambda i, j: (0, 0, 0))
    in_specs = [
        pl.BlockSpec((1, tm, D_MODEL), lambda i, j: (i, j, 0)),
        pl.BlockSpec((1, 1, D_MODEL), mod_map),
        pl.BlockSpec((1, 1, D_MODEL), mod_map),
        pl.BlockSpec((1, D_MODEL), lambda i, j: (0, 0)),
        pl.BlockSpec((D_MODEL, IN_WIDTH), lambda i, j: (0, 0)),
        pl.BlockSpec((4, LANES), lambda i, j: (0, 0)),
        pl.BlockSpec((MXU_DIM, MXU_DIM), lambda i, j: (0, 0)),
    ]
    args = [x, shift, scale, gain, w_in_bf, head_gains, ones_bd]
    if rope:
        in_specs += [pl.BlockSpec((tm, LANES), lambda i, j: (j, 0))] * 2
        args += list(rope_tables)
    kv_spec = pl.BlockSpec((1, tm, KV_W), lambda i, j: (i, j, 0))
    out_specs = [
        pl.BlockSpec((1, N_HEADS, tm, LANES), lambda i, j: (i, 0, j, 0)),
        pl.BlockSpec((1, tm, MIX_WIDTH), lambda i, j: (i, j, 0)),
        kv_spec, kv_spec, kv_spec, kv_spec,
    ]
    out_shape = [
        jax.ShapeDtypeStruct((b, N_HEADS, n, LANES), BF16),
        jax.ShapeDtypeStruct((b, n, MIX_WIDTH), BF16),
    ] + [jax.ShapeDtypeStruct((b, n, KV_W), BF16)] * 4
    if emit_f32:
        out_specs += [kv_spec] * 4
        out_shape += [jax.ShapeDtypeStruct((b, n, KV_W), F32)] * 4
    return pl.pallas_call(
        functools.partial(_proj_kernel, rope=rope, emit_f32=emit_f32),
        grid=(b, n // tm),
        in_specs=in_specs,
        out_specs=out_specs,
        out_shape=out_shape,
        compiler_params=pltpu.CompilerParams(
            dimension_semantics=("arbitrary", "arbitrary"), vmem_limit_bytes=VMEM_LIMIT_BYTES),
        name="project_rope" if rope else "project_ctx",
    )(*args)


def _softmax_pv(parts, sink=None):
    m = None
    for s, _ in parts:
        mi = jnp.max(s, axis=-1, keepdims=True)
        m = mi if m is None else jnp.maximum(m, mi)
    if sink is not None:
        m = jnp.maximum(m, sink)
    l = None
    o = None
    for s, v in parts:
        p = jnp.exp(s - m)
        li = jnp.sum(p, axis=-1, keepdims=True)
        oi = _dot(p.astype(BF16), v)
        l = li if l is None else l + li
        o = oi if o is None else o + oi
    if sink is not None:
        l = l + jnp.exp(sink - m)
    return o * (1.0 / l)


def _attn_kernel(*refs, n_ctx, windowed, n_lat, tq):
    it = iter(refs)
    q_ref, g_ref, x_ref, gate_ref, ka_ref, va_ref, kb_ref, vb_ref = (next(it) for _ in range(8))
    if n_ctx:
        cka_ref, cva_ref, ckb_ref, cvb_ref = (next(it) for _ in range(4))
    sink_ref, wout_ref, out_ref, o_scr = (next(it) for _ in range(4))

    t = pl.program_id(1)

    def head_a(h, carry):
        q = q_ref[0, h]
        parts = []
        if n_ctx:
            parts.append((_dot_t(q, cka_ref[0]), cva_ref[0]))
        parts.append((_dot_t(q, ka_ref[0]), va_ref[0]))
        o_scr[h] = _softmax_pv(parts)
        return carry

    lax.fori_loop(0, N_HEADS_A, head_a, 0)

    if windowed:
        band = tq + 2 * WINDOW
        start = jnp.clip(t * tq - WINDOW, 0, n_lat - band)
        start = pl.multiple_of(start, WINDOW)
        qpos = t * tq + lax.broadcasted_iota(jnp.int32, (tq, band), 0)
        kpos = start + lax.broadcasted_iota(jnp.int32, (tq, band), 1)
        bias = jnp.where(jnp.abs(kpos - qpos) <= WINDOW, 0.0, NEG_INF).astype(F32)

    def head_b(hb, carry):
        h = N_HEADS_A + hb
        q = q_ref[0, h]
        parts = []
        if n_ctx:
            parts.append((_dot_t(q, ckb_ref[0]), cvb_ref[0]))
        if windowed:
            kb = kb_ref[0, pl.ds(start, band), :]
            vb = vb_ref[0, pl.ds(start, band), :]
            parts.append((_dot_t(q, kb) + bias, vb))
        else:
            parts.append((_dot_t(q, kb_ref[0]), vb_ref[0]))
        o_scr[h] = _softmax_pv(parts, sink=sink_ref[hb])
        return carry

    lax.fori_loop(0, N_HEADS_B, head_b, 0)

    lane = lax.broadcasted_iota(jnp.int32, (1, LANES), 1)
    chunks = []
    for c in range(N_HEADS // 2):
        kv = ((2 * c) % N_HEADS_A) // GROUP
        a = o_scr[2 * c]
        b = o_scr[2 * c + 1]
        if kv == 0:
            b = pltpu.roll(b, HEAD_DIM, 1)
        else:
            a = pltpu.roll(a, HEAD_DIM, 1)
        chunks.append(jnp.where(lane < HEAD_DIM, a, b))
    o = jnp.concatenate(chunks, axis=1)
    gated = (o * g_ref[0].astype(F32)).astype(BF16)
    y = _dot(gated, wout_ref[...])
    out_ref[0] = x_ref[0] + gate_ref[0] * y


def _attend(q, g, x, gate, ka, va, kb, vb, ctx, sink, w_out_bf, windowed):
    b, n, _ = x.shape
    tq = TOKEN_TILE
    n_ctx = 0 if ctx is None else ctx[0].shape[1]
    per_batch = gate.shape[0] != 1
    gate_map = (lambda i, j: (i, 0, 0)) if per_batch else (lambda i, j: (0, 0, 0))
    tile = lambda i, j: (i, j, 0)
    whole = lambda i, j: (i, 0, 0)
    in_specs = [
        pl.BlockSpec((1, N_HEADS, tq, LANES), lambda i, j: (i, 0, j, 0)),
        pl.BlockSpec((1, tq, MIX_WIDTH), tile),
        pl.BlockSpec((1, tq, D_MODEL), tile),
        pl.BlockSpec((1, 1, D_MODEL), gate_map),
    ] + [pl.BlockSpec((1, n, KV_W), whole)] * 4
    args = [q, g, x, gate, ka, va, kb, vb]
    if n_ctx:
        in_specs += [pl.BlockSpec((1, n_ctx, KV_W), whole)] * 4
        args += list(ctx)
    in_specs += [
        pl.BlockSpec(memory_space=pltpu.SMEM),
        pl.BlockSpec((MIX_WIDTH, D_MODEL), lambda i, j: (0, 0)),
    ]
    args += [sink, w_out_bf]
    return pl.pallas_call(
        functools.partial(_attn_kernel, n_ctx=n_ctx, windowed=windowed, n_lat=n, tq=tq),
        grid=(b, n // tq),
        in_specs=in_specs,
        out_specs=pl.BlockSpec((1, tq, D_MODEL), tile),
        out_shape=jax.ShapeDtypeStruct((b, n, D_MODEL), F32),
        scratch_shapes=[pltpu.VMEM((N_HEADS, tq, LANES), F32)],
        compiler_params=pltpu.CompilerParams(
            dimension_semantics=("arbitrary", "arbitrary"), vmem_limit_bytes=VMEM_LIMIT_BYTES),
        name="attend_latent" if windowed else "attend_ctx",
    )(*args)


def _rope_tables(n_tokens):
    rows = n_tokens // GRID_W
    row = jnp.repeat(jnp.arange(rows, dtype=F32), GRID_W)
    col = jnp.tile(jnp.arange(GRID_W, dtype=F32), rows)
    n_freq = HEAD_DIM // 4
    inv = ROPE_THETA ** (-jnp.arange(n_freq, dtype=F32) / n_freq)
    ar = row[:, None] * inv[None, :]
    ac = col[:, None] * inv[None, :]
    ang = jnp.concatenate([ar, ar, ac, ac], axis=-1)
    sign = jnp.asarray(np.tile(np.repeat([-1.0, 1.0], 16), HEAD_DIM // 32), F32)
    cos = jnp.tile(jnp.cos(ang), (1, LANES // HEAD_DIM))
    sin_signed = jnp.tile(jnp.sin(ang) * sign[None, :], (1, LANES // HEAD_DIM))
    return cos, sin_signed


def _ones_blockdiag():
    idx = np.arange(MXU_DIM) // HEAD_DIM
    return jnp.asarray(idx[:, None] == idx[None, :], BF16)


def kernel(x_prompt, x_sample, cache_k_a, cache_v_a, cache_k_b, cache_v_b, c, c_ctx,
           w_mod, b_mod, norm_gain, w_in, qn_a, kn_a, qn_b, kn_b, sink_b, w_out):
    depth = w_in.shape[0]
    batch, seq, _ = x_prompt.shape
    dec_batch, dec_seq, _ = x_sample.shape
    past = cache_k_a.shape[2]

    rope_tables = _rope_tables(dec_seq)
    ones_bd = _ones_blockdiag()
    n_cond = 1 + dec_batch
    cond_rows = -(-n_cond // 8) * 8
    cond = jnp.concatenate(
        [c_ctx[None, :], c, jnp.zeros((cond_rows - n_cond, D_MODEL), F32)], axis=0)

    xp, xs = x_prompt, x_sample
    new_kv = [[], [], [], []]
    tile2 = lambda v: jnp.tile(v, LANES // HEAD_DIM)
    for l in range(depth):
        w_in_bf = w_in[l].astype(BF16)
        w_out_bf = w_out[l].astype(BF16)
        q_scale = HEAD_DIM ** -0.5
        head_gains = jnp.stack([tile2(qn_a[l]) * q_scale, tile2(kn_a[l]),
                                tile2(qn_b[l]) * q_scale, tile2(kn_b[l])])
        gain = norm_gain[l].reshape(1, D_MODEL)
        sink = sink_b[l].astype(F32)

        m = _modulation(cond, w_mod[l], b_mod[l])
        shift, scale, gate = (m[:, i * D_MODEL:(i + 1) * D_MODEL] for i in range(3))

        sel = lambda v: v[0:1].reshape(1, 1, D_MODEL)
        q, g, ka, va, kb, vb, ka32, va32, kb32, vb32 = _project(
            xp, sel(shift), sel(scale), gain, w_in_bf, head_gains, ones_bd, None, True)
        xp = _attend(q, g, xp, sel(gate), ka, va, kb, vb, None, sink, w_out_bf, False)
        for acc, v in zip(new_kv, (ka32, va32, kb32, vb32)):
            acc.append(v.reshape(batch, seq, N_KV_A, HEAD_DIM))

        sel = lambda v: v[1:n_cond].reshape(dec_batch, 1, D_MODEL)
        q, g, ka, va, kb, vb = _project(
            xs, sel(shift), sel(scale), gain, w_in_bf, head_gains, ones_bd, rope_tables, False)
        ctx = tuple(cache[:, l].reshape(dec_batch, past, KV_W).astype(BF16)
                    for cache in (cache_k_a, cache_v_a, cache_k_b, cache_v_b))
        xs = _attend(q, g, xs, sel(gate), ka, va, kb, vb, ctx, sink, w_out_bf, True)

    return (xp, xs) + tuple(jnp.stack(v, axis=1) for v in new_kv)
```

```python
import functools

import numpy as np
import jax
import jax.numpy as jnp
from jax import lax
from jax.experimental import pallas as pl
from jax.experimental.pallas import tpu as pltpu

F32 = jnp.float32
BF16 = jnp.bfloat16

D_MODEL = 1024
HEAD_DIM = 64
N_HEADS_A = 8
N_KV_A = 2
N_HEADS_B = 8
N_KV_B = 2
N_HEADS = N_HEADS_A + N_HEADS_B
GROUP = N_HEADS_A // N_KV_A
WIDTH_A = N_HEADS_A * HEAD_DIM
WIDTH_B = N_HEADS_B * HEAD_DIM
MIX_WIDTH = WIDTH_A + WIDTH_B
KV_W = N_KV_A * HEAD_DIM
IN_WIDTH = 2 * (2 * WIDTH_A + 2 * KV_W)
GRID_W = 64
WINDOW = 128
ROPE_THETA = 10000.0
EPS = 1e-6
NEG_INF = -1e30

LANES = 128
MXU_DIM = 256
VMEM_LIMIT_BYTES = 56 * 1024 * 1024

_OFF_QA = 0
_OFF_KA = _OFF_QA + WIDTH_A
_OFF_VA = _OFF_KA + KV_W
_OFF_GA = _OFF_VA + KV_W
_OFF_QB = _OFF_GA + WIDTH_A
_OFF_KB = _OFF_QB + WIDTH_B
_OFF_VB = _OFF_KB + KV_W
_OFF_GB = _OFF_VB + KV_W

TOKEN_TILE = 256
HEAD_UNROLL = 2


def _dot(a, b):
    return jnp.dot(a, b, preferred_element_type=F32)


def _dot_t(a, b):
    return lax.dot_general(a, b, (((1,), (1,)), ((), ())), preferred_element_type=F32)


def _mod_kernel(cond_ref, w_ref, b_ref, out_ref):
    c = cond_ref[...]
    s = c * jax.nn.sigmoid(c)
    out_ref[...] = _dot(s.astype(BF16), w_ref[...].astype(BF16)) + b_ref[...]


def _modulation(cond, w_mod, b_mod):
    rows = cond.shape[0]
    n_out = w_mod.shape[1]
    bn = 768
    return pl.pallas_call(
        _mod_kernel,
        grid=(n_out // bn,),
        in_specs=[
            pl.BlockSpec((rows, D_MODEL), lambda j: (0, 0)),
            pl.BlockSpec((D_MODEL, bn), lambda j: (0, j)),
            pl.BlockSpec((1, bn), lambda j: (0, j)),
        ],
        out_specs=pl.BlockSpec((rows, bn), lambda j: (0, j)),
        out_shape=jax.ShapeDtypeStruct((rows, n_out), F32),
        compiler_params=pltpu.CompilerParams(
            dimension_semantics=("arbitrary",), vmem_limit_bytes=VMEM_LIMIT_BYTES),
        name="modulation",
    )(cond, w_mod, b_mod.reshape(1, n_out))


def _head_rms(blk, ones_blockdiag):
    ss = _dot((blk * blk).astype(BF16), ones_blockdiag)
    return blk * lax.rsqrt(ss * (1.0 / HEAD_DIM) + EPS)


def _rope(blk, cos, sin_signed, low_half):
    up = pltpu.roll(blk, LANES - 16, 1)
    down = pltpu.roll(blk, 16, 1)
    return blk * cos + jnp.where(low_half, up, down) * sin_signed


def _proj_kernel(*refs, rope, emit_f32):
    it = iter(refs)
    x_ref, shift_ref, scale_ref, gain_ref, w_ref, hg_ref, ones_ref = (next(it) for _ in range(7))
    if rope:
        cos_ref, sin_ref = next(it), next(it)
    q_ref, g_ref, ka_ref, va_ref, kb_ref, vb_ref = (next(it) for _ in range(6))
    if emit_f32:
        ka32_ref, va32_ref, kb32_ref, vb32_ref = (next(it) for _ in range(4))

    x = x_ref[0]
    ms = jnp.mean(x * x, axis=-1, keepdims=True)
    h = x * lax.rsqrt(ms + EPS) * gain_ref[...]
    h = h * (1.0 + scale_ref[0]) + shift_ref[0]
    hb = h.astype(BF16)

    lane = lax.broadcasted_iota(jnp.int32, (1, LANES), 1)
    if rope:
        cos = cos_ref[...]
        sin_signed = sin_ref[...]
        low_half = (lane % 32) < 16

    ones256 = ones_ref[...]
    ones128 = ones_ref[0:LANES, 0:LANES]

    def seg(off, width):
        return _dot(hb, w_ref[:, off:off + width])

    def normed_chunks(off, width, gain):
        p = seg(off, width)
        out = []
        step = MXU_DIM if width >= MXU_DIM else LANES
        for c0 in range(0, width, step):
            y = _head_rms(p[:, c0:c0 + step], ones256 if step == MXU_DIM else ones128)
            for c1 in range(0, step, LANES):
                yc = y[:, c1:c1 + LANES] * gain
                if rope:
                    yc = _rope(yc, cos, sin_signed, low_half)
                out.append(yc)
        return out

    def store_q(chunks, head0, gain_unused=None):
        for hh in range(2 * len(chunks)):
            kv = hh // GROUP
            c = chunks[hh // 2]
            if hh % 2 != kv:
                c = pltpu.roll(c, HEAD_DIM, 1)
            keep = (lane < HEAD_DIM) if kv == 0 else (lane >= HEAD_DIM)
            q_ref[0, head0 + hh] = jnp.where(keep, c, 0.0).astype(BF16)

    def silu(v):
        return v * jax.nn.sigmoid(v)

    store_q(normed_chunks(_OFF_QA, WIDTH_A, hg_ref[0:1, :]), 0)
    ka = normed_chunks(_OFF_KA, KV_W, hg_ref[1:2, :])[0]
    va = seg(_OFF_VA, KV_W)
    ka_ref[0] = ka.astype(BF16)
    va_ref[0] = va.astype(BF16)
    g_ref[0, :, 0:WIDTH_A] = silu(seg(_OFF_GA, WIDTH_A)).astype(BF16)
    store_q(normed_chunks(_OFF_QB, WIDTH_B, hg_ref[2:3, :]), N_HEADS_A)
    kb = normed_chunks(_OFF_KB, KV_W, hg_ref[3:4, :])[0]
    vb = seg(_OFF_VB, KV_W)
    kb_ref[0] = kb.astype(BF16)
    vb_ref[0] = vb.astype(BF16)
    g_ref[0, :, WIDTH_A:MIX_WIDTH] = silu(seg(_OFF_GB, WIDTH_B)).astype(BF16)
    if emit_f32:
        ka32_ref[0] = ka
        va32_ref[0] = va
        kb32_ref[0] = kb
        vb32_ref[0] = vb


def _project(x, shift, scale, gain, w_in_bf, head_gains, ones_bd, rope_tables, emit_f32):
    b, n, _ = x.shape
    tm = TOKEN_TILE
    rope = rope_tables is not None
    per_batch = shift.shape[0] != 1
    mod_map = (lambda i, j: (i, 0, 0)) if per_batch else (lambda i, j: (0, 0, 0))
    in_specs = [
        pl.BlockSpec((1, tm, D_MODEL), lambda i, j: (i, j, 0)),
        pl.BlockSpec((1, 1, D_MODEL), mod_map),
        pl.BlockSpec((1, 1, D_MODEL), mod_map),
        pl.BlockSpec((1, D_MODEL), lambda i, j: (0, 0)),
        pl.BlockSpec((D_MODEL, IN_WIDTH), lambda i, j: (0, 0)),
        pl.BlockSpec((4, LANES), lambda i, j: (0, 0)),
        pl.BlockSpec((MXU_DIM, MXU_DIM), lambda i, j: (0, 0)),
    ]
    args = [x, shift, scale, gain, w_in_bf, head_gains, ones_bd]
    if rope:
        in_specs += [pl.BlockSpec((tm, LANES), lambda i, j: (j, 0))] * 2
        args += list(rope_tables)
    kv_spec = pl.BlockSpec((1, tm, KV_W), lambda i, j: (i, j, 0))
    out_specs = [
        pl.BlockSpec((1, N_HEADS, tm, LANES), lambda i, j: (i, 0, j, 0)),
        pl.BlockSpec((1, tm, MIX_WIDTH), lambda i, j: (i, j, 0)),
        kv_spec, kv_spec, kv_spec, kv_spec,
    ]
    out_shape = [
        jax.ShapeDtypeStruct((b, N_HEADS, n, LANES), BF16),
        jax.ShapeDtypeStruct((b, n, MIX_WIDTH), BF16),
    ] + [jax.ShapeDtypeStruct((b, n, KV_W), BF16)] * 4
    if emit_f32:
        out_specs += [kv_spec] * 4
        out_shape += [jax.ShapeDtypeStruct((b, n, KV_W), F32)] * 4
    return pl.pallas_call(
        functools.partial(_proj_kernel, rope=rope, emit_f32=emit_f32),
        grid=(b, n // tm),
        in_specs=in_specs,
        out_specs=out_specs,
        out_shape=out_shape,
        compiler_params=pltpu.CompilerParams(
            dimension_semantics=("arbitrary", "arbitrary"), vmem_limit_bytes=VMEM_LIMIT_BYTES),
        name="project_rope" if rope else "project_ctx",
    )(*args)


def _softmax_pv(parts, sink=None):
    m = None
    for s, _ in parts:
        mi = jnp.max(s, axis=-1, keepdims=True)
        m = mi if m is None else jnp.maximum(m, mi)
    if sink is not None:
        m = jnp.maximum(m, sink)
    l = None
    o = None
    for s, v in parts:
        p = jnp.exp(s - m)
        li = jnp.sum(p, axis=-1, keepdims=True)
        oi = _dot(p.astype(BF16), v)
        l = li if l is None else l + li
        o = oi if o is None else o + oi
    if sink is not None:
        l = l + jnp.exp(sink - m)
    return o * (1.0 / l)


def _attn_kernel(*refs, n_ctx, windowed, n_lat, tq):
    it = iter(refs)
    q_ref, g_ref, x_ref, gate_ref, ka_ref, va_ref, kb_ref, vb_ref = (next(it) for _ in range(8))
    if n_ctx:
        cka_ref, cva_ref, ckb_ref, cvb_ref = (next(it) for _ in range(4))
    sink_ref, wout_ref, out_ref, o_scr = (next(it) for _ in range(4))

    t = pl.program_id(1)

    def head_a(h, carry):
        q = q_ref[0, h]
        parts = []
        if n_ctx:
            parts.append((_dot_t(q, cka_ref[0]), cva_ref[0]))
        parts.append((_dot_t(q, ka_ref[0]), va_ref[0]))
        o_scr[h] = _softmax_pv(parts)
        return carry

    lax.fori_loop(0, N_HEADS_A, head_a, 0, unroll=HEAD_UNROLL)

    if windowed:
        band = tq + 2 * WINDOW
        start = jnp.clip(t * tq - WINDOW, 0, n_lat - band)
        start = pl.multiple_of(start, WINDOW)
        qpos = t * tq + lax.broadcasted_iota(jnp.int32, (tq, band), 0)
        kpos = start + lax.broadcasted_iota(jnp.int32, (tq, band), 1)
        bias = jnp.where(jnp.abs(kpos - qpos) <= WINDOW, 0.0, NEG_INF).astype(F32)

    def head_b(hb, carry):
        h = N_HEADS_A + hb
        q = q_ref[0, h]
        parts = []
        if n_ctx:
            parts.append((_dot_t(q, ckb_ref[0]), cvb_ref[0]))
        if windowed:
            kb = kb_ref[0, pl.ds(start, band), :]
            vb = vb_ref[0, pl.ds(start, band), :]
            parts.append((_dot_t(q, kb) + bias, vb))
        else:
            parts.append((_dot_t(q, kb_ref[0]), vb_ref[0]))
        o_scr[h] = _softmax_pv(parts, sink=sink_ref[hb])
        return carry

    lax.fori_loop(0, N_HEADS_B, head_b, 0, unroll=HEAD_UNROLL)

    lane = lax.broadcasted_iota(jnp.int32, (1, LANES), 1)
    chunks = []
    for c in range(N_HEADS // 2):
        kv = ((2 * c) % N_HEADS_A) // GROUP
        a = o_scr[2 * c]
        b = o_scr[2 * c + 1]
        if kv == 0:
            b = pltpu.roll(b, HEAD_DIM, 1)
        else:
            a = pltpu.roll(a, HEAD_DIM, 1)
        chunks.append(jnp.where(lane < HEAD_DIM, a, b))
    o = jnp.concatenate(chunks, axis=1)
    gated = (o * g_ref[0].astype(F32)).astype(BF16)
    y = _dot(gated, wout_ref[...])
    out_ref[0] = x_ref[0] + gate_ref[0] * y


def _attend(q, g, x, gate, ka, va, kb, vb, ctx, sink, w_out_bf, windowed):
    b, n, _ = x.shape
    tq = TOKEN_TILE
    n_ctx = 0 if ctx is None else ctx[0].shape[1]
    per_batch = gate.shape[0] != 1
    gate_map = (lambda i, j: (i, 0, 0)) if per_batch else (lambda i, j: (0, 0, 0))
    tile = lambda i, j: (i, j, 0)
    whole = lambda i, j: (i, 0, 0)
    in_specs = [
        pl.BlockSpec((1, N_HEADS, tq, LANES), lambda i, j: (i, 0, j, 0)),
        pl.BlockSpec((1, tq, MIX_WIDTH), tile),
        pl.BlockSpec((1, tq, D_MODEL), tile),
        pl.BlockSpec((1, 1, D_MODEL), gate_map),
    ] + [pl.BlockSpec((1, n, KV_W), whole)] * 4
    args = [q, g, x, gate, ka, va, kb, vb]
    if n_ctx:
        in_specs += [pl.BlockSpec((1, n_ctx, KV_W), whole)] * 4
        args += list(ctx)
    in_specs += [
        pl.BlockSpec(memory_space=pltpu.SMEM),
        pl.BlockSpec((MIX_WIDTH, D_MODEL), lambda i, j: (0, 0)),
    ]
    args += [sink, w_out_bf]
    return pl.pallas_call(
        functools.partial(_attn_kernel, n_ctx=n_ctx, windowed=windowed, n_lat=n, tq=tq),
        grid=(b, n // tq),
        in_specs=in_specs,
        out_specs=pl.BlockSpec((1, tq, D_MODEL), tile),
        out_shape=jax.ShapeDtypeStruct((b, n, D_MODEL), F32),
        scratch_shapes=[pltpu.VMEM((N_HEADS, tq, LANES), F32)],
        compiler_params=pltpu.CompilerParams(
            dimension_semantics=("arbitrary", "arbitrary"), vmem_limit_bytes=VMEM_LIMIT_BYTES),
        name="attend_latent" if windowed else "attend_ctx",
    )(*args)


def _rope_tables(n_tokens):
    rows = n_tokens // GRID_W
    row = jnp.repeat(jnp.arange(rows, dtype=F32), GRID_W)
    col = jnp.tile(jnp.arange(GRID_W, dtype=F32), rows)
    n_freq = HEAD_DIM // 4
    inv = ROPE_THETA ** (-jnp.arange(n_freq, dtype=F32) / n_freq)
    ar = row[:, None] * inv[None, :]
    ac = col[:, None] * inv[None, :]
    ang = jnp.concatenate([ar, ar, ac, ac], axis=-1)
    sign = jnp.asarray(np.tile(np.repeat([-1.0, 1.0], 16), HEAD_DIM // 32), F32)
    cos = jnp.tile(jnp.cos(ang), (1, LANES // HEAD_DIM))
    sin_signed = jnp.tile(jnp.sin(ang) * sign[None, :], (1, LANES // HEAD_DIM))
    return cos, sin_signed


def _ones_blockdiag():
    idx = np.arange(MXU_DIM) // HEAD_DIM
    return jnp.asarray(idx[:, None] == idx[None, :], BF16)


def kernel(x_prompt, x_sample, cache_k_a, cache_v_a, cache_k_b, cache_v_b, c, c_ctx,
           w_mod, b_mod, norm_gain, w_in, qn_a, kn_a, qn_b, kn_b, sink_b, w_out):
    depth = w_in.shape[0]
    batch, seq, _ = x_prompt.shape
    dec_batch, dec_seq, _ = x_sample.shape
    past = cache_k_a.shape[2]

    rope_tables = _rope_tables(dec_seq)
    ones_bd = _ones_blockdiag()
    n_cond = 1 + dec_batch
    cond_rows = -(-n_cond // 8) * 8
    cond = jnp.concatenate(
        [c_ctx[None, :], c, jnp.zeros((cond_rows - n_cond, D_MODEL), F32)], axis=0)

    xp, xs = x_prompt, x_sample
    new_kv = [[], [], [], []]
    tile2 = lambda v: jnp.tile(v, LANES // HEAD_DIM)
    for l in range(depth):
        w_in_bf = w_in[l].astype(BF16)
        w_out_bf = w_out[l].astype(BF16)
        q_scale = HEAD_DIM ** -0.5
        head_gains = jnp.stack([tile2(qn_a[l]) * q_scale, tile2(kn_a[l]),
                                tile2(qn_b[l]) * q_scale, tile2(kn_b[l])])
        gain = norm_gain[l].reshape(1, D_MODEL)
        sink = sink_b[l].astype(F32)

        m = _modulation(cond, w_mod[l], b_mod[l])
        shift, scale, gate = (m[:, i * D_MODEL:(i + 1) * D_MODEL] for i in range(3))

        sel = lambda v: v[0:1].reshape(1, 1, D_MODEL)
        q, g, ka, va, kb, vb, ka32, va32, kb32, vb32 = _project(
            xp, sel(shift), sel(scale), gain, w_in_bf, head_gains, ones_bd, None, True)
        xp = _attend(q, g, xp, sel(gate), ka, va, kb, vb, None, sink, w_out_bf, False)
        for acc, v in zip(new_kv, (ka32, va32, kb32, vb32)):
            acc.append(v.reshape(batch, seq, N_KV_A, HEAD_DIM))

        sel = lambda v: v[1:n_cond].reshape(dec_batch, 1, D_MODEL)
        q, g, ka, va, kb, vb = _project(
            xs, sel(shift), sel(scale), gain, w_in_bf, head_gains, ones_bd, rope_tables, False)
        ctx = tuple(cache[:, l].reshape(dec_batch, past, KV_W).astype(BF16)
                    for cache in (cache_k_a, cache_v_a, cache_k_b, cache_v_b))
        xs = _attend(q, g, xs, sel(gate), ka, va, kb, vb, ctx, sink, w_out_bf, True)

    return (xp, xs) + tuple(jnp.stack(v, axis=1) for v in new_kv)
```

```python
import functools

import numpy as np
import jax
import jax.numpy as jnp
from jax import lax
from jax.experimental import pallas as pl
from jax.experimental.pallas import tpu as pltpu

F32 = jnp.float32
BF16 = jnp.bfloat16

D_MODEL = 1024
HEAD_DIM = 64
N_HEADS_A = 8
N_KV_A = 2
N_HEADS_B = 8
N_KV_B = 2
N_HEADS = N_HEADS_A + N_HEADS_B
GROUP = N_HEADS_A // N_KV_A
WIDTH_A = N_HEADS_A * HEAD_DIM
WIDTH_B = N_HEADS_B * HEAD_DIM
MIX_WIDTH = WIDTH_A + WIDTH_B
KV_W = N_KV_A * HEAD_DIM
IN_WIDTH = 2 * (2 * WIDTH_A + 2 * KV_W)
GRID_W = 64
WINDOW = 128
ROPE_THETA = 10000.0
EPS = 1e-6
NEG_INF = -1e30

LANES = 128
MXU_DIM = 256
VMEM_LIMIT_BYTES = 56 * 1024 * 1024

_OFF_QA = 0
_OFF_KA = _OFF_QA + WIDTH_A
_OFF_VA = _OFF_KA + KV_W
_OFF_GA = _OFF_VA + KV_W
_OFF_QB = _OFF_GA + WIDTH_A
_OFF_KB = _OFF_QB + WIDTH_B
_OFF_VB = _OFF_KB + KV_W
_OFF_GB = _OFF_VB + KV_W

TOKEN_TILE = 256
KEY_CHUNK = 256


def _dot(a, b):
    return jnp.dot(a, b, preferred_element_type=F32)


def _dot_t(a, b):
    return lax.dot_general(a, b, (((1,), (1,)), ((), ())), preferred_element_type=F32)


def _mod_kernel(cond_ref, w_ref, b_ref, out_ref):
    c = cond_ref[...]
    s = c * jax.nn.sigmoid(c)
    out_ref[...] = _dot(s.astype(BF16), w_ref[...].astype(BF16)) + b_ref[...]


def _modulation(cond, w_mod, b_mod):
    rows = cond.shape[0]
    n_out = w_mod.shape[1]
    bn = 768
    return pl.pallas_call(
        _mod_kernel,
        grid=(n_out // bn,),
        in_specs=[
            pl.BlockSpec((rows, D_MODEL), lambda j: (0, 0)),
            pl.BlockSpec((D_MODEL, bn), lambda j: (0, j)),
            pl.BlockSpec((1, bn), lambda j: (0, j)),
        ],
        out_specs=pl.BlockSpec((rows, bn), lambda j: (0, j)),
        out_shape=jax.ShapeDtypeStruct((rows, n_out), F32),
        compiler_params=pltpu.CompilerParams(
            dimension_semantics=("arbitrary",), vmem_limit_bytes=VMEM_LIMIT_BYTES),
        name="modulation",
    )(cond, w_mod, b_mod.reshape(1, n_out))


def _head_rms(blk, ones_blockdiag):
    ss = _dot((blk * blk).astype(BF16), ones_blockdiag)
    return blk * lax.rsqrt(ss * (1.0 / HEAD_DIM) + EPS)


def _rope(blk, cos, sin_signed, low_half):
    up = pltpu.roll(blk, LANES - 16, 1)
    down = pltpu.roll(blk, 16, 1)
    return blk * cos + jnp.where(low_half, up, down) * sin_signed


def _proj_kernel(*refs, rope, emit_f32):
    it = iter(refs)
    x_ref, shift_ref, scale_ref, gain_ref, w_ref, hg_ref, ones_ref = (next(it) for _ in range(7))
    if rope:
        cos_ref, sin_ref = next(it), next(it)
    q_ref, g_ref, ka_ref, vta_ref, kb_ref, vtb_ref = (next(it) for _ in range(6))
    if emit_f32:
        ka32_ref, va32_ref, kb32_ref, vb32_ref = (next(it) for _ in range(4))

    x = x_ref[0]
    ms = jnp.mean(x * x, axis=-1, keepdims=True)
    h = x * lax.rsqrt(ms + EPS) * gain_ref[...]
    h = h * (1.0 + scale_ref[0]) + shift_ref[0]
    hb = h.astype(BF16)

    lane = lax.broadcasted_iota(jnp.int32, (1, LANES), 1)
    if rope:
        cos = cos_ref[...]
        sin_signed = sin_ref[...]
        low_half = (lane % 32) < 16

    ones256 = ones_ref[...]
    ones128 = ones_ref[0:LANES, 0:LANES]

    def seg(off, width):
        return _dot(hb, w_ref[:, off:off + width])

    def normed_chunks(off, width, gain):
        p = seg(off, width)
        out = []
        step = MXU_DIM if width >= MXU_DIM else LANES
        for c0 in range(0, width, step):
            y = _head_rms(p[:, c0:c0 + step], ones256 if step == MXU_DIM else ones128)
            for c1 in range(0, step, LANES):
                yc = y[:, c1:c1 + LANES] * gain
                if rope:
                    yc = _rope(yc, cos, sin_signed, low_half)
                out.append(yc)
        return out

    def store_q(chunks, head0):
        for hh in range(2 * len(chunks)):
            kv = hh // GROUP
            c = chunks[hh // 2]
            if hh % 2 != kv:
                c = pltpu.roll(c, HEAD_DIM, 1)
            keep = (lane < HEAD_DIM) if kv == 0 else (lane >= HEAD_DIM)
            q_ref[0, head0 + hh] = jnp.where(keep, c, 0.0).astype(BF16)

    def silu(v):
        return v * jax.nn.sigmoid(v)

    store_q(normed_chunks(_OFF_QA, WIDTH_A, hg_ref[0:1, :]), 0)
    ka = normed_chunks(_OFF_KA, KV_W, hg_ref[1:2, :])[0]
    va = seg(_OFF_VA, KV_W)
    ka_ref[0] = ka.astype(BF16)
    vta_ref[0] = va.T.astype(BF16)
    g_ref[0, :, 0:WIDTH_A] = silu(seg(_OFF_GA, WIDTH_A)).astype(BF16)
    store_q(normed_chunks(_OFF_QB, WIDTH_B, hg_ref[2:3, :]), N_HEADS_A)
    kb = normed_chunks(_OFF_KB, KV_W, hg_ref[3:4, :])[0]
    vb = seg(_OFF_VB, KV_W)
    kb_ref[0] = kb.astype(BF16)
    vtb_ref[0] = vb.T.astype(BF16)
    g_ref[0, :, WIDTH_A:MIX_WIDTH] = silu(seg(_OFF_GB, WIDTH_B)).astype(BF16)
    if emit_f32:
        ka32_ref[0] = ka
        va32_ref[0] = va
        kb32_ref[0] = kb
        vb32_ref[0] = vb


def _project(x, shift, scale, gain, w_in_bf, head_gains, ones_bd, rope_tables, emit_f32):
    b, n, _ = x.shape
    tm = TOKEN_TILE
    rope = rope_tables is not None
    per_batch = shift.shape[0] != 1
    mod_map = (lambda i, j: (i, 0, 0)) if per_batch else (lambda i, j: (0, 0, 0))
    in_specs = [
        pl.BlockSpec((1, tm, D_MODEL), lambda i, j: (i, j, 0)),
        pl.BlockSpec((1, 1, D_MODEL), mod_map),
        pl.BlockSpec((1, 1, D_MODEL), mod_map),
        pl.BlockSpec((1, D_MODEL), lambda i, j: (0, 0)),
        pl.BlockSpec((D_MODEL, IN_WIDTH), lambda i, j: (0, 0)),
        pl.BlockSpec((4, LANES), lambda i, j: (0, 0)),
        pl.BlockSpec((MXU_DIM, MXU_DIM), lambda i, j: (0, 0)),
    ]
    args = [x, shift, scale, gain, w_in_bf, head_gains, ones_bd]
    if rope:
        in_specs += [pl.BlockSpec((tm, LANES), lambda i, j: (j, 0))] * 2
        args += list(rope_tables)
    k_spec = pl.BlockSpec((1, tm, KV_W), lambda i, j: (i, j, 0))
    vt_spec = pl.BlockSpec((1, KV_W, tm), lambda i, j: (i, 0, j))
    out_specs = [
        pl.BlockSpec((1, N_HEADS, tm, LANES), lambda i, j: (i, 0, j, 0)),
        pl.BlockSpec((1, tm, MIX_WIDTH), lambda i, j: (i, j, 0)),
        k_spec, vt_spec, k_spec, vt_spec,
    ]
    k_shape = jax.ShapeDtypeStruct((b, n, KV_W), BF16)
    vt_shape = jax.ShapeDtypeStruct((b, KV_W, n), BF16)
    out_shape = [
        jax.ShapeDtypeStruct((b, N_HEADS, n, LANES), BF16),
        jax.ShapeDtypeStruct((b, n, MIX_WIDTH), BF16),
        k_shape, vt_shape, k_shape, vt_shape,
    ]
    if emit_f32:
        out_specs += [k_spec] * 4
        out_shape += [jax.ShapeDtypeStruct((b, n, KV_W), F32)] * 4
    return pl.pallas_call(
        functools.partial(_proj_kernel, rope=rope, emit_f32=emit_f32),
        grid=(b, n // tm),
        in_specs=in_specs,
        out_specs=out_specs,
        out_shape=out_shape,
        compiler_params=pltpu.CompilerParams(
            dimension_semantics=("arbitrary", "arbitrary"), vmem_limit_bytes=VMEM_LIMIT_BYTES),
        name="project_rope" if rope else "project_ctx",
    )(*args)


def _attn_kernel(*refs, n_ctx, windowed, n_lat, tq, gb):
    it = iter(refs)
    q_ref, g_ref, x_ref, gate_ref, ka_ref, vta_ref, kb_ref, vtb_ref = (next(it) for _ in range(8))
    if n_ctx:
        cka_ref, cva_ref, ckb_ref, cvb_ref = (next(it) for _ in range(4))
    sink_ref, wout_ref, out_ref = (next(it) for _ in range(3))
    s_scr, m_scr, o_scr = (next(it) for _ in range(3))

    t = pl.program_id(1)
    width = gb * tq

    if n_ctx:
        ctx_k = {"a": cka_ref[0].astype(BF16), "b": ckb_ref[0].astype(BF16)}
        ctx_vt = {"a": cva_ref[0].T.astype(BF16), "b": cvb_ref[0].T.astype(BF16)}

    if windowed:
        band = tq + 2 * WINDOW
        start = jnp.clip(t * tq - WINDOW, 0, n_lat - band)
        start = pl.multiple_of(start, WINDOW)
        kpos = start + lax.broadcasted_iota(jnp.int32, (band, tq), 0)
        qpos = t * tq + lax.broadcasted_iota(jnp.int32, (band, tq), 1)
        bias = jnp.where(jnp.abs(kpos - qpos) <= WINDOW, 0.0, NEG_INF).astype(F32)
        bias = jnp.concatenate([bias] * gb, axis=1)
        n_keys_b = n_ctx + band
    else:
        n_keys_b = n_ctx + n_lat
    n_keys_a = n_ctx + n_lat

    ck = KEY_CHUNK

    def chunks(mixer, kv):
        rows = slice(kv * HEAD_DIM, (kv + 1) * HEAD_DIM)
        out = []
        for r in range(0, n_ctx, ck):
            out.append((r,
                        functools.partial(lambda r: ctx_k[mixer][r:r + ck, :], r),
                        functools.partial(lambda r: ctx_vt[mixer][rows, r:r + ck], r), None))
        if mixer == "a":
            for r in range(0, n_lat, ck):
                out.append((n_ctx + r,
                            functools.partial(lambda r: ka_ref[0, r:r + ck, :], r),
                            functools.partial(lambda r: vta_ref[0, rows, r:r + ck], r), None))
        elif windowed:
            for r in range(0, band, ck):
                out.append((n_ctx + r,
                            functools.partial(lambda r: kb_ref[0, pl.ds(start + r, ck), :], r),
                            functools.partial(lambda r: vtb_ref[0, rows, pl.ds(start + r, ck)], r),
                            bias[r:r + ck, :]))
        else:
            for r in range(0, n_lat, ck):
                out.append((n_ctx + r,
                            functools.partial(lambda r: kb_ref[0, r:r + ck, :], r),
                            functools.partial(lambda r: vtb_ref[0, rows, r:r + ck], r), None))
        return out

    def sink_row(h0):
        return jnp.concatenate(
            [jnp.full((1, tq), sink_ref[h0 - N_HEADS_A + j], F32) for j in range(gb)], axis=1)

    def scores_steps(mixer, h0, slot):
        kv = (h0 % N_HEADS_A) // GROUP
        state = {"m": None}
        todo = chunks(mixer, kv)

        def step(idx):
            row, load_k, _, kbias = todo[idx]
            qg = q_ref[0, h0:h0 + gb].reshape(width, LANES)
            s = _dot_t(load_k(), qg)
            if kbias is not None:
                s = s + kbias
            s_scr[slot, row:row + ck, :] = s
            mi = jnp.max(s, axis=0, keepdims=True)
            m = mi if state["m"] is None else jnp.maximum(state["m"], mi)
            if idx == len(todo) - 1:
                if mixer == "b":
                    m = jnp.maximum(m, sink_row(h0))
                m_scr[slot] = m
            state["m"] = m

        return [functools.partial(step, i) for i in range(len(todo))]

    def softmax_pv_steps(mixer, h0, slot):
        kv = (h0 % N_HEADS_A) // GROUP
        state = {"l": None, "o": None}
        todo = chunks(mixer, kv)

        def step(idx):
            row, _, load_vt, _ = todo[idx]
            m = m_scr[slot]
            p = jnp.exp(s_scr[slot, row:row + ck, :] - m)
            li = jnp.sum(p, axis=0, keepdims=True)
            oi = _dot(load_vt(), p.astype(BF16))
            l = li if state["l"] is None else state["l"] + li
            o = oi if state["o"] is None else state["o"] + oi
            if idx == len(todo) - 1:
                if mixer == "b":
                    l = l + jnp.exp(sink_row(h0) - m)
                o = o * (1.0 / l)
                for j in range(gb):
                    o_scr[(h0 + j) * HEAD_DIM:(h0 + j + 1) * HEAD_DIM, :] = o[:, j * tq:(j + 1) * tq]
            state["l"], state["o"] = l, o

        return [functools.partial(step, i) for i in range(len(todo))]

    tasks = ([("a", h0) for h0 in range(0, N_HEADS_A, gb)]
             + [("b", N_HEADS_A + h0) for h0 in range(0, N_HEADS_B, gb)])
    for i in range(len(tasks) + 1):
        drain = softmax_pv_steps(*tasks[i - 1], (i - 1) % 2) if i >= 1 else []
        fill = scores_steps(*tasks[i], i % 2) if i < len(tasks) else []
        for c in range(max(len(drain), len(fill))):
            if c < len(fill):
                fill[c]()
            if c < len(drain):
                drain[c]()

    o = o_scr[...].T
    gated = (o * g_ref[0].astype(F32)).astype(BF16)
    y = _dot(gated, wout_ref[...])
    out_ref[0] = x_ref[0] + gate_ref[0] * y


def _attend(q, g, x, gate, ka, vta, kb, vtb, ctx, sink, w_out_bf, windowed, gb):
    b, n, _ = x.shape
    tq = TOKEN_TILE
    n_ctx = 0 if ctx is None else ctx[0].shape[1]
    per_batch = gate.shape[0] != 1
    gate_map = (lambda i, j: (i, 0, 0)) if per_batch else (lambda i, j: (0, 0, 0))
    tile = lambda i, j: (i, j, 0)
    whole = lambda i, j: (i, 0, 0)
    k_spec = pl.BlockSpec((1, n, KV_W), whole)
    vt_spec = pl.BlockSpec((1, KV_W, n), whole)
    in_specs = [
        pl.BlockSpec((1, N_HEADS, tq, LANES), lambda i, j: (i, 0, j, 0)),
        pl.BlockSpec((1, tq, MIX_WIDTH), tile),
        pl.BlockSpec((1, tq, D_MODEL), tile),
        pl.BlockSpec((1, 1, D_MODEL), gate_map),
        k_spec, vt_spec, k_spec, vt_spec,
    ]
    args = [q, g, x, gate, ka, vta, kb, vtb]
    if n_ctx:
        in_specs += [pl.BlockSpec((1, n_ctx, KV_W), whole)] * 4
        args += list(ctx)
    in_specs += [
        pl.BlockSpec(memory_space=pltpu.SMEM),
        pl.BlockSpec((MIX_WIDTH, D_MODEL), lambda i, j: (0, 0)),
    ]
    args += [sink, w_out_bf]
    n_keys = n_ctx + n
    return pl.pallas_call(
        functools.partial(_attn_kernel, n_ctx=n_ctx, windowed=windowed, n_lat=n, tq=tq, gb=gb),
        grid=(b, n // tq),
        in_specs=in_specs,
        out_specs=pl.BlockSpec((1, tq, D_MODEL), tile),
        out_shape=jax.ShapeDtypeStruct((b, n, D_MODEL), F32),
        scratch_shapes=[
            pltpu.VMEM((2, n_keys, gb * tq), F32),
            pltpu.VMEM((2, 1, gb * tq), F32),
            pltpu.VMEM((MIX_WIDTH, tq), F32),
        ],
        compiler_params=pltpu.CompilerParams(
            dimension_semantics=("arbitrary", "arbitrary"), vmem_limit_bytes=VMEM_LIMIT_BYTES),
        name="attend_latent" if windowed else "attend_ctx",
    )(*args)


def _rope_tables(n_tokens):
    rows = n_tokens // GRID_W
    row = jnp.repeat(jnp.arange(rows, dtype=F32), GRID_W)
    col = jnp.tile(jnp.arange(GRID_W, dtype=F32), rows)
    n_freq = HEAD_DIM // 4
    inv = ROPE_THETA ** (-jnp.arange(n_freq, dtype=F32) / n_freq)
    ar = row[:, None] * inv[None, :]
    ac = col[:, None] * inv[None, :]
    ang = jnp.concatenate([ar, ar, ac, ac], axis=-1)
    sign = jnp.asarray(np.tile(np.repeat([-1.0, 1.0], 16), HEAD_DIM // 32), F32)
    cos = jnp.tile(jnp.cos(ang), (1, LANES // HEAD_DIM))
    sin_signed = jnp.tile(jnp.sin(ang) * sign[None, :], (1, LANES // HEAD_DIM))
    return cos, sin_signed


def _ones_blockdiag():
    idx = np.arange(MXU_DIM) // HEAD_DIM
    return jnp.asarray(idx[:, None] == idx[None, :], BF16)


def kernel(x_prompt, x_sample, cache_k_a, cache_v_a, cache_k_b, cache_v_b, c, c_ctx,
           w_mod, b_mod, norm_gain, w_in, qn_a, kn_a, qn_b, kn_b, sink_b, w_out):
    depth = w_in.shape[0]
    batch, seq, _ = x_prompt.shape
    dec_batch, dec_seq, _ = x_sample.shape
    past = cache_k_a.shape[2]

    rope_tables = _rope_tables(dec_seq)
    ones_bd = _ones_blockdiag()
    n_cond = 1 + dec_batch
    cond_rows = -(-n_cond // 8) * 8
    cond = jnp.concatenate(
        [c_ctx[None, :], c, jnp.zeros((cond_rows - n_cond, D_MODEL), F32)], axis=0)

    xp, xs = x_prompt, x_sample
    new_kv = [[], [], [], []]
    tile2 = lambda v: jnp.tile(v, LANES // HEAD_DIM)
    for l in range(depth):
        w_in_bf = w_in[l].astype(BF16)
        w_out_bf = w_out[l].astype(BF16)
        q_scale = HEAD_DIM ** -0.5
        head_gains = jnp.stack([tile2(qn_a[l]) * q_scale, tile2(kn_a[l]),
                                tile2(qn_b[l]) * q_scale, tile2(kn_b[l])])
        gain = norm_gain[l].reshape(1, D_MODEL)
        sink = sink_b[l].astype(F32)

        m = _modulation(cond, w_mod[l], b_mod[l])
        shift, scale, gate = (m[:, i * D_MODEL:(i + 1) * D_MODEL] for i in range(3))

        sel = lambda v: v[0:1].reshape(1, 1, D_MODEL)
        q, g, ka, vta, kb, vtb, ka32, va32, kb32, vb32 = _project(
            xp, sel(shift), sel(scale), gain, w_in_bf, head_gains, ones_bd, None, True)
        xp = _attend(q, g, xp, sel(gate), ka, vta, kb, vtb, None, sink, w_out_bf, False, GROUP)
        for acc, v in zip(new_kv, (ka32, va32, kb32, vb32)):
            acc.append(v.reshape(batch, seq, N_KV_A, HEAD_DIM))

        sel = lambda v: v[1:n_cond].reshape(dec_batch, 1, D_MODEL)
        q, g, ka, vta, kb, vtb = _project(
            xs, sel(shift), sel(scale), gain, w_in_bf, head_gains, ones_bd, rope_tables, False)
        ctx = tuple(cache[:, l].reshape(dec_batch, past, KV_W)
                    for cache in (cache_k_a, cache_v_a, cache_k_b, cache_v_b))
        xs = _attend(q, g, xs, sel(gate), ka, vta, kb, vtb, ctx, sink, w_out_bf, True, 1)

    return (xp, xs) + tuple(jnp.stack(v, axis=1) for v in new_kv)
```

```python
import functools

import numpy as np
import jax
import jax.numpy as jnp
from jax import lax
from jax.experimental import pallas as pl
from jax.experimental.pallas import tpu as pltpu

F32 = jnp.float32
BF16 = jnp.bfloat16

D_MODEL = 1024
HEAD_DIM = 64
N_HEADS_A = 8
N_KV_A = 2
N_HEADS_B = 8
N_KV_B = 2
N_HEADS = N_HEADS_A + N_HEADS_B
GROUP = N_HEADS_A // N_KV_A
WIDTH_A = N_HEADS_A * HEAD_DIM
WIDTH_B = N_HEADS_B * HEAD_DIM
MIX_WIDTH = WIDTH_A + WIDTH_B
KV_W = N_KV_A * HEAD_DIM
IN_WIDTH = 2 * (2 * WIDTH_A + 2 * KV_W)
GRID_W = 64
WINDOW = 128
ROPE_THETA = 10000.0
EPS = 1e-6
NEG_INF = -1e30
LOG2E = 1.4426950408889634

LANES = 128
MXU_DIM = 256
BF16_ROWS = 16
VMEM_LIMIT_BYTES = 56 * 1024 * 1024

_OFF_QA = 0
_OFF_KA = _OFF_QA + WIDTH_A
_OFF_VA = _OFF_KA + KV_W
_OFF_GA = _OFF_VA + KV_W
_OFF_QB = _OFF_GA + WIDTH_A
_OFF_KB = _OFF_QB + WIDTH_B
_OFF_VB = _OFF_KB + KV_W
_OFF_GB = _OFF_VB + KV_W

TOKEN_TILE = 256
PROJ_TILE = 1024
KEY_CHUNK = 256
CTX_TILES_PER_STEP = 4
LATENT_TILES_PER_STEP = 2
FILL_AHEAD = 3


def _dot(a, b):
    return jnp.dot(a, b, preferred_element_type=F32)


def _dot_t(a, b):
    return lax.dot_general(a, b, (((1,), (1,)), ((), ())), preferred_element_type=F32)


def _order_after(x, token):
    zero = (pltpu.bitcast(token, jnp.uint32) >> 16) >> 16
    bits = pltpu.bitcast(x, jnp.uint32)
    zero = jnp.concatenate([zero[0:1, :]] * (bits.shape[1] // LANES), axis=1)
    return pltpu.bitcast(bits | jnp.broadcast_to(zero, bits.shape), x.dtype)


def _mod_kernel(cond_ref, w_ref, b_ref, out_ref):
    c = cond_ref[...]
    s = c * jax.nn.sigmoid(c)
    out_ref[...] = _dot(s.astype(BF16), w_ref[...].astype(BF16)) + b_ref[...]


def _modulation(cond, w_mod, b_mod):
    rows = cond.shape[0]
    n_out = w_mod.shape[1]
    bn = 768
    return pl.pallas_call(
        _mod_kernel,
        grid=(n_out // bn,),
        in_specs=[
            pl.BlockSpec((rows, D_MODEL), lambda j: (0, 0)),
            pl.BlockSpec((D_MODEL, bn), lambda j: (0, j)),
            pl.BlockSpec((1, bn), lambda j: (0, j)),
        ],
        out_specs=pl.BlockSpec((rows, bn), lambda j: (0, j)),
        out_shape=jax.ShapeDtypeStruct((rows, n_out), F32),
        compiler_params=pltpu.CompilerParams(
            dimension_semantics=("arbitrary",), vmem_limit_bytes=VMEM_LIMIT_BYTES),
        name="modulation",
    )(cond, w_mod, b_mod.reshape(1, n_out))


def _head_rms(blk, ones_blockdiag):
    ss = _dot((blk * blk).astype(BF16), ones_blockdiag)
    return blk * lax.rsqrt(ss * (1.0 / HEAD_DIM) + EPS)


def _rope(blk, cos, sin_signed, low_half):
    up = pltpu.roll(blk, LANES - 16, 1)
    down = pltpu.roll(blk, 16, 1)
    return blk * cos + jnp.where(low_half, up, down) * sin_signed


def _proj_kernel(*refs, rope, emit_f32):
    it = iter(refs)
    x_ref, shift_ref, scale_ref, gain_ref, w_ref, hg_ref, ones_ref = (next(it) for _ in range(7))
    if rope:
        cos_ref, sin_ref = next(it), next(it)
    q_ref, g_ref, ka_ref, vta_ref, kb_ref, vtb_ref = (next(it) for _ in range(6))
    if emit_f32:
        ka32_ref, va32_ref, kb32_ref, vb32_ref = (next(it) for _ in range(4))

    x = x_ref[0]
    ms = jnp.mean(x * x, axis=-1, keepdims=True)
    h = x * lax.rsqrt(ms + EPS) * gain_ref[...]
    h = h * (1.0 + scale_ref[0]) + shift_ref[0]
    hb = h.astype(BF16)

    lane = lax.broadcasted_iota(jnp.int32, (1, LANES), 1)
    if rope:
        cos = cos_ref[...]
        sin_signed = sin_ref[...]
        low_half = (lane % 32) < 16

    ones256 = ones_ref[...]

    def seg(off, width):
        return _dot(hb, w_ref[:, off:off + width])

    def normed_chunks(p, gains):
        out = []
        for c0 in range(0, p.shape[1], MXU_DIM):
            y = _head_rms(p[:, c0:c0 + MXU_DIM], ones256)
            for c1 in range(0, MXU_DIM, LANES):
                yc = y[:, c1:c1 + LANES] * gains[(c0 + c1) // LANES]
                if rope:
                    yc = _rope(yc, cos, sin_signed, low_half)
                out.append(yc)
        return out

    def store_q(chunks, head0):
        for hh in range(2 * len(chunks)):
            kv = hh // GROUP
            c = chunks[hh // 2]
            if hh % 2 != kv:
                c = pltpu.roll(c, HEAD_DIM, 1)
            keep = (lane < HEAD_DIM) if kv == 0 else (lane >= HEAD_DIM)
            q_ref[0, head0 + hh] = jnp.where(keep, c, 0.0).astype(BF16)

    def silu(v):
        return v * jax.nn.sigmoid(v)

    p_qa = seg(_OFF_QA, WIDTH_A)
    p_qb = seg(_OFF_QB, WIDTH_B)
    qa = normed_chunks(p_qa, [hg_ref[0:1, :]] * (WIDTH_A // LANES))
    p_kva = seg(_OFF_KA, 2 * KV_W)
    store_q(qa, 0)
    qb = normed_chunks(p_qb, [hg_ref[2:3, :]] * (WIDTH_B // LANES))
    p_kvb = seg(_OFF_KB, 2 * KV_W)
    store_q(qb, N_HEADS_A)
    k_both = jnp.concatenate([p_kva[:, 0:KV_W], p_kvb[:, 0:KV_W]], axis=1)
    ka, kb = normed_chunks(k_both, [hg_ref[1:2, :], hg_ref[3:4, :]])
    p_ga = seg(_OFF_GA, WIDTH_A)
    p_gb = seg(_OFF_GB, WIDTH_B)
    ka_ref[0] = ka.astype(BF16)
    kb_ref[0] = kb.astype(BF16)
    va = p_kva[:, KV_W:2 * KV_W]
    vb = p_kvb[:, KV_W:2 * KV_W]
    vta_ref[0] = va.T.astype(BF16)
    vtb_ref[0] = vb.T.astype(BF16)
    g_ref[0, :, 0:WIDTH_A] = silu(p_ga).astype(BF16)
    g_ref[0, :, WIDTH_A:MIX_WIDTH] = silu(p_gb).astype(BF16)
    if emit_f32:
        for ref, val in ((ka32_ref, ka), (va32_ref, va), (kb32_ref, kb), (vb32_ref, vb)):
            ref[...] = val.reshape(ref.shape)


def _project(x, shift, scale, gain, w_in_bf, head_gains, ones_bd, rope_tables, emit_f32):
    b, n, _ = x.shape
    tm = min(PROJ_TILE, n)
    rope = rope_tables is not None
    per_batch = shift.shape[0] != 1
    mod_map = (lambda i, j: (i, 0, 0)) if per_batch else (lambda i, j: (0, 0, 0))
    in_specs = [
        pl.BlockSpec((1, tm, D_MODEL), lambda i, j: (i, j, 0)),
        pl.BlockSpec((1, 1, D_MODEL), mod_map),
        pl.BlockSpec((1, 1, D_MODEL), mod_map),
        pl.BlockSpec((1, D_MODEL), lambda i, j: (0, 0)),
        pl.BlockSpec((D_MODEL, IN_WIDTH), lambda i, j: (0, 0)),
        pl.BlockSpec((4, LANES), lambda i, j: (0, 0)),
        pl.BlockSpec((MXU_DIM, MXU_DIM), lambda i, j: (0, 0)),
    ]
    args = [x, shift, scale, gain, w_in_bf, head_gains, ones_bd]
    if rope:
        in_specs += [pl.BlockSpec((tm, LANES), lambda i, j: (j, 0))] * 2
        args += list(rope_tables)
    k_spec = pl.BlockSpec((1, tm, KV_W), lambda i, j: (i, j, 0))
    vt_spec = pl.BlockSpec((1, KV_W, tm), lambda i, j: (i, 0, j))
    out_specs = [
        pl.BlockSpec((1, N_HEADS, tm, LANES), lambda i, j: (i, 0, j, 0)),
        pl.BlockSpec((1, tm, MIX_WIDTH), lambda i, j: (i, j, 0)),
        k_spec, vt_spec, k_spec, vt_spec,
    ]
    k_shape = jax.ShapeDtypeStruct((b, n, KV_W), BF16)
    vt_shape = jax.ShapeDtypeStruct((b, KV_W, n), BF16)
    out_shape = [
        jax.ShapeDtypeStruct((b, N_HEADS, n, LANES), BF16),
        jax.ShapeDtypeStruct((b, n, MIX_WIDTH), BF16),
        k_shape, vt_shape, k_shape, vt_shape,
    ]
    if emit_f32:
        req = emit_f32
        out_specs += [pl.BlockSpec((tm // req, req, KV_W),
                                   lambda i, j: (i * (n // tm) + j, 0, 0))] * 4
        out_shape += [jax.ShapeDtypeStruct((b * n // req, req, KV_W), F32)] * 4
    return pl.pallas_call(
        functools.partial(_proj_kernel, rope=rope, emit_f32=bool(emit_f32)),
        grid=(b, n // tm),
        in_specs=in_specs,
        out_specs=out_specs,
        out_shape=out_shape,
        compiler_params=pltpu.CompilerParams(
            dimension_semantics=("arbitrary", "arbitrary"), vmem_limit_bytes=VMEM_LIMIT_BYTES),
        name="project_rope" if rope else "project_ctx",
    )(*args)


def _attn_kernel(*refs, n_ctx, windowed, n_lat, tq, gb, sub, own_keys, exp_lead):
    it = iter(refs)
    q_ref, g_ref, x_ref, gate_ref, ka_ref, vta_ref, kb_ref, vtb_ref = (next(it) for _ in range(8))
    if n_ctx:
        cka_ref, cva_ref, ckb_ref, cvb_ref = (next(it) for _ in range(4))
    sink_ref, wout_ref, out_ref = (next(it) for _ in range(3))
    o_scr = next(it)

    width = gb * tq

    if n_ctx:
        ctx_k = {"a": cka_ref[0].astype(BF16), "b": ckb_ref[0].astype(BF16)}
        ctx_vt = {"a": cva_ref[0].T.astype(BF16), "b": cvb_ref[0].T.astype(BF16)}

    band = tq + 2 * WINDOW

    def band_of(u):
        t = pl.program_id(1) * sub + u
        start = jnp.clip(t * tq - WINDOW, 0, n_lat - band)
        start = pl.multiple_of(start, WINDOW)
        kpos = start + lax.broadcasted_iota(jnp.int32, (band, tq), 0)
        qpos = t * tq + lax.broadcasted_iota(jnp.int32, (band, tq), 1)
        bias = jnp.where(jnp.abs(kpos - qpos) <= WINDOW, 0.0, NEG_INF).astype(F32)
        return start, jnp.concatenate([bias] * gb, axis=1)

    ck = KEY_CHUNK

    def spans(total):
        return [(r, min(ck, total - r)) for r in range(0, total, ck)]

    def chunks(u, mixer, kv):
        rows = slice(kv * HEAD_DIM, (kv + 1) * HEAD_DIM)
        k0 = u * n_lat if own_keys else 0
        out = []
        for r, n in spans(n_ctx):
            out.append((n,
                        functools.partial(lambda r, n: ctx_k[mixer][r:r + n, :], r, n),
                        functools.partial(lambda r, n: ctx_vt[mixer][rows, r:r + n], r, n), None))
        if mixer == "a":
            for r, n in spans(n_lat):
                r += k0
                out.append((n,
                            functools.partial(lambda r, n: ka_ref[0, r:r + n, :], r, n),
                            functools.partial(lambda r, n: vta_ref[0, rows, r:r + n], r, n), None))
        elif windowed:
            start, bias = band_of(u)
            for r, n in spans(band):
                out.append((n,
                            functools.partial(
                                lambda r, n: kb_ref[0, pl.ds(k0 + start + r, n), :], r, n),
                            functools.partial(
                                lambda r, n: vtb_ref[0, rows, pl.ds(k0 + start + r, n)], r, n),
                            bias[r:r + n, :]))
        else:
            for r, n in spans(n_lat):
                r += k0
                out.append((n,
                            functools.partial(lambda r, n: kb_ref[0, r:r + n, :], r, n),
                            functools.partial(lambda r, n: vtb_ref[0, rows, r:r + n], r, n), None))
        return out

    def sink_row(h0):
        return jnp.concatenate(
            [jnp.full((1, tq), sink_ref[h0 - N_HEADS_A + j] * LOG2E, F32) for j in range(gb)],
            axis=1)

    tasks = ([("a", h0) for h0 in range(0, N_HEADS_A, gb)]
             + [("b", N_HEADS_A + h0) for h0 in range(0, N_HEADS_B, gb)])
    items = []
    for u in range(sub):
        for ti, (mixer, h0) in enumerate(tasks):
            todo = chunks(u, mixer, (h0 % N_HEADS_A) // GROUP)
            for idx, chunk in enumerate(todo):
                last = idx == len(todo) - 1
                items.append((u, mixer, h0, chunk, idx == 0, last, last and ti == len(tasks) - 1))

    scores = {}
    running = {}
    tokens = []

    def emit_scores(k):
        u, mixer, h0, (_, load_k, _, kbias), _, _, _ = items[k]
        qg = q_ref[0, h0:h0 + gb, u * tq:(u + 1) * tq, :].reshape(width, LANES)
        s = _dot_t(load_k(), qg)
        scores[k] = s if kbias is None else s + kbias

    def emit_merge(u):
        toks = slice(u * tq, (u + 1) * tq)
        o = o_scr[u].T
        gated = (o * g_ref[0, toks, :].astype(F32)).astype(BF16)
        y = _dot(gated, wout_ref[...])
        out_ref[0, toks, :] = x_ref[0, toks, :] + gate_ref[0] * y

    def emit_softmax(k):
        u, mixer, h0, (nk, _, load_vt, _), first, last, tile_done = items[k]
        s = scores.pop(k)
        m_new = jnp.max(s, axis=0, keepdims=True)
        if first:
            if mixer == "b":
                m_new = jnp.maximum(m_new, sink_row(h0))
        else:
            m_old, o_old = running.pop((u, h0))
            m_new = jnp.maximum(m_old, m_new)
        if k >= exp_lead:
            m_new = _order_after(m_new, tokens[k - exp_lead])
        p = jnp.exp2(s - m_new)
        vt_ones = jnp.concatenate([load_vt(), jnp.ones((BF16_ROWS, nk), BF16)], axis=0)
        o = _dot(vt_ones, p.astype(BF16))
        if not first:
            o = o_old * jnp.exp2(m_old - m_new) + o
        tokens.append(o[0:8, 0:LANES])
        if not last:
            running[(u, h0)] = (m_new, o)
            return
        l = o[HEAD_DIM:HEAD_DIM + 1, :]
        if mixer == "b":
            l = l + jnp.exp2(sink_row(h0) - m_new)
        o = o[0:HEAD_DIM, :] * (1.0 / l)
        for j in range(gb):
            o_scr[u, (h0 + j) * HEAD_DIM:(h0 + j + 1) * HEAD_DIM, :] = o[:, j * tq:(j + 1) * tq]
        if tile_done:
            emit_merge(u)

    for k in range(len(items) + FILL_AHEAD):
        if k < len(items):
            emit_scores(k)
        if k >= FILL_AHEAD:
            emit_softmax(k - FILL_AHEAD)


def _attend(q, g, x, gate, ka, vta, kb, vtb, ctx, sink, w_out_bf, windowed, gb, sub, exp_lead,
            folded=1):
    b, n_all, _ = x.shape
    n = n_all // folded
    tq = min(TOKEN_TILE, n)
    nt = n // tq
    n_ctx = 0 if ctx is None else ctx[0].shape[1]
    own_keys = folded > 1
    if own_keys:
        assert b == 1 and nt == 1 and folded % sub == 0 and gate.shape[0] == 1
        grid = (folded // sub, 1)
        row = lambda i, j: 0
        tile = lambda i, j: i
        n_kblock = sub * n
        kblock = lambda i, j: i
    else:
        assert nt % sub == 0
        grid = (b, nt // sub)
        row = lambda i, j: i
        tile = lambda i, j: j
        n_kblock = n
        kblock = lambda i, j: 0
    per_batch = gate.shape[0] != 1
    gate_map = (lambda i, j: (i, 0, 0)) if per_batch else (lambda i, j: (0, 0, 0))
    tok_map = lambda i, j: (row(i, j), tile(i, j), 0)
    k_spec = pl.BlockSpec((1, n_kblock, KV_W), lambda i, j: (row(i, j), kblock(i, j), 0))
    vt_spec = pl.BlockSpec((1, KV_W, n_kblock), lambda i, j: (row(i, j), 0, kblock(i, j)))
    in_specs = [
        pl.BlockSpec((1, N_HEADS, sub * tq, LANES), lambda i, j: (row(i, j), 0, tile(i, j), 0)),
        pl.BlockSpec((1, sub * tq, MIX_WIDTH), tok_map),
        pl.BlockSpec((1, sub * tq, D_MODEL), tok_map),
        pl.BlockSpec((1, 1, D_MODEL), gate_map),
        k_spec, vt_spec, k_spec, vt_spec,
    ]
    args = [q, g, x, gate, ka, vta, kb, vtb]
    if n_ctx:
        in_specs += [pl.BlockSpec((1, n_ctx, KV_W), lambda i, j: (i, 0, 0))] * 4
        args += list(ctx)
    in_specs += [
        pl.BlockSpec(memory_space=pltpu.SMEM),
        pl.BlockSpec((MIX_WIDTH, D_MODEL), lambda i, j: (0, 0)),
    ]
    args += [sink, w_out_bf]
    return pl.pallas_call(
        functools.partial(_attn_kernel, n_ctx=n_ctx, windowed=windowed, n_lat=n, tq=tq, gb=gb,
                          sub=sub, own_keys=own_keys, exp_lead=exp_lead),
        grid=grid,
        in_specs=in_specs,
        out_specs=pl.BlockSpec((1, sub * tq, D_MODEL), tok_map),
        out_shape=jax.ShapeDtypeStruct((b, n_all, D_MODEL), F32),
        scratch_shapes=[
            pltpu.VMEM((sub, MIX_WIDTH, tq), F32),
        ],
        compiler_params=pltpu.CompilerParams(
            dimension_semantics=("arbitrary", "arbitrary"), vmem_limit_bytes=VMEM_LIMIT_BYTES),
        name="attend_latent" if windowed else "attend_ctx",
    )(*args)


def _rope_tables(n_tokens):
    rows = n_tokens // GRID_W
    row = jnp.repeat(jnp.arange(rows, dtype=F32), GRID_W)
    col = jnp.tile(jnp.arange(GRID_W, dtype=F32), rows)
    n_freq = HEAD_DIM // 4
    inv = ROPE_THETA ** (-jnp.arange(n_freq, dtype=F32) / n_freq)
    ar = row[:, None] * inv[None, :]
    ac = col[:, None] * inv[None, :]
    ang = jnp.concatenate([ar, ar, ac, ac], axis=-1)
    sign = jnp.asarray(np.tile(np.repeat([-1.0, 1.0], 16), HEAD_DIM // 32), F32)
    cos = jnp.tile(jnp.cos(ang), (1, LANES // HEAD_DIM))
    sin_signed = jnp.tile(jnp.sin(ang) * sign[None, :], (1, LANES // HEAD_DIM))
    return cos, sin_signed


def _ones_blockdiag():
    idx = np.arange(MXU_DIM) // HEAD_DIM
    return jnp.asarray(idx[:, None] == idx[None, :], BF16)


def kernel(x_prompt, x_sample, cache_k_a, cache_v_a, cache_k_b, cache_v_b, c, c_ctx,
           w_mod, b_mod, norm_gain, w_in, qn_a, kn_a, qn_b, kn_b, sink_b, w_out):
    depth = w_in.shape[0]
    batch, seq, _ = x_prompt.shape
    dec_batch, dec_seq, _ = x_sample.shape
    past = cache_k_a.shape[2]

    rope_tables = _rope_tables(dec_seq)
    ones_bd = _ones_blockdiag()
    n_cond = 1 + dec_batch
    cond_rows = -(-n_cond // 8) * 8
    cond = jnp.concatenate(
        [c_ctx[None, :], c, jnp.zeros((cond_rows - n_cond, D_MODEL), F32)], axis=0)

    xp, xs = x_prompt, x_sample
    new_kv = [[], [], [], []]
    tile2 = lambda v: jnp.tile(v, LANES // HEAD_DIM)
    for l in range(depth):
        w_in_bf = w_in[l].astype(BF16)
        w_out_bf = w_out[l].astype(BF16)
        q_scale = HEAD_DIM ** -0.5 * LOG2E
        head_gains = jnp.stack([tile2(qn_a[l]) * q_scale, tile2(kn_a[l]),
                                tile2(qn_b[l]) * q_scale, tile2(kn_b[l])])
        gain = norm_gain[l].reshape(1, D_MODEL)
        sink = sink_b[l].astype(F32)

        m = _modulation(cond, w_mod[l], b_mod[l])
        shift, scale, gate = (m[:, i * D_MODEL:(i + 1) * D_MODEL] for i in range(3))

        sel = lambda v: v[0:1].reshape(1, 1, D_MODEL)
        xp_flat = xp.reshape(1, batch * seq, D_MODEL)
        q, g, ka, vta, kb, vtb, ka32, va32, kb32, vb32 = _project(
            xp_flat, sel(shift), sel(scale), gain, w_in_bf, head_gains, ones_bd, None, seq)
        xp = _attend(q, g, xp_flat, sel(gate), ka, vta, kb, vtb, None, sink, w_out_bf, False,
                     GROUP, CTX_TILES_PER_STEP, 3, folded=batch).reshape(batch, seq, D_MODEL)
        for acc, v in zip(new_kv, (ka32, va32, kb32, vb32)):
            acc.append(v.reshape(batch, seq, N_KV_A, HEAD_DIM))

        sel = lambda v: v[1:n_cond].reshape(dec_batch, 1, D_MODEL)
        q, g, ka, vta, kb, vtb = _project(
            xs, sel(shift), sel(scale), gain, w_in_bf, head_gains, ones_bd, rope_tables, None)
        ctx = tuple(cache[:, l].reshape(dec_batch, past, KV_W)
                    for cache in (cache_k_a, cache_v_a, cache_k_b, cache_v_b))
        xs = _attend(q, g, xs, sel(gate), ka, vta, kb, vtb, ctx, sink, w_out_bf, True, 2,
                     LATENT_TILES_PER_STEP, 2)

    return (xp, xs) + tuple(jnp.stack(v, axis=1) for v in new_kv)
```

```python
import functools

import numpy as np
import jax
import jax.numpy as jnp
from jax import lax
from jax.experimental import pallas as pl
from jax.experimental.pallas import tpu as pltpu

F32 = jnp.float32
BF16 = jnp.bfloat16

D_MODEL = 1024
HEAD_DIM = 64
N_HEADS_A = 8
N_KV_A = 2
N_HEADS_B = 8
N_KV_B = 2
N_HEADS = N_HEADS_A + N_HEADS_B
GROUP = N_HEADS_A // N_KV_A
WIDTH_A = N_HEADS_A * HEAD_DIM
WIDTH_B = N_HEADS_B * HEAD_DIM
MIX_WIDTH = WIDTH_A + WIDTH_B
KV_W = N_KV_A * HEAD_DIM
IN_WIDTH = 2 * (2 * WIDTH_A + 2 * KV_W)
GRID_W = 64
WINDOW = 128
ROPE_THETA = 10000.0
EPS = 1e-6
NEG_INF = -1e30
LOG2E = 1.4426950408889634

LANES = 128
MXU_DIM = 256
BF16_ROWS = 16
VMEM_LIMIT_BYTES = 56 * 1024 * 1024

_OFF_QA = 0
_OFF_KA = _OFF_QA + WIDTH_A
_OFF_VA = _OFF_KA + KV_W
_OFF_GA = _OFF_VA + KV_W
_OFF_QB = _OFF_GA + WIDTH_A
_OFF_KB = _OFF_QB + WIDTH_B
_OFF_VB = _OFF_KB + KV_W
_OFF_GB = _OFF_VB + KV_W

TOKEN_TILE = 256
PROJ_TILE = 1024
KEY_CHUNK = 256
CTX_TILES_PER_STEP = 4
LATENT_TILES_PER_STEP = 2
FILL_AHEAD = 3


def _dot(a, b):
    return jnp.dot(a, b, preferred_element_type=F32)


def _dot_t(a, b):
    return lax.dot_general(a, b, (((1,), (1,)), ((), ())), preferred_element_type=F32)


def _order_after(x, token):
    zero = (pltpu.bitcast(token, jnp.uint32) >> 16) >> 16
    bits = pltpu.bitcast(x, jnp.uint32)
    zero = jnp.concatenate([zero[0:1, :]] * (bits.shape[1] // LANES), axis=1)
    return pltpu.bitcast(bits | jnp.broadcast_to(zero, bits.shape), x.dtype)


def _mod_kernel(cond_ref, w_ref, b_ref, out_ref):
    c = cond_ref[...]
    s = c * jax.nn.sigmoid(c)
    out_ref[...] = _dot(s.astype(BF16), w_ref[...].astype(BF16)) + b_ref[...]


def _modulation(cond, w_mod, b_mod):
    rows = cond.shape[0]
    n_out = w_mod.shape[1]
    bn = 768
    return pl.pallas_call(
        _mod_kernel,
        grid=(n_out // bn,),
        in_specs=[
            pl.BlockSpec((rows, D_MODEL), lambda j: (0, 0)),
            pl.BlockSpec((D_MODEL, bn), lambda j: (0, j)),
            pl.BlockSpec((1, bn), lambda j: (0, j)),
        ],
        out_specs=pl.BlockSpec((rows, bn), lambda j: (0, j)),
        out_shape=jax.ShapeDtypeStruct((rows, n_out), F32),
        compiler_params=pltpu.CompilerParams(
            dimension_semantics=("arbitrary",), vmem_limit_bytes=VMEM_LIMIT_BYTES),
        name="modulation",
    )(cond, w_mod, b_mod.reshape(1, n_out))


def _head_rms(blk, ones_blockdiag):
    ss = _dot((blk * blk).astype(BF16), ones_blockdiag)
    return blk * lax.rsqrt(ss * (1.0 / HEAD_DIM) + EPS)


def _rope(blk, cos, sin_signed, low_half):
    up = pltpu.roll(blk, LANES - 16, 1)
    down = pltpu.roll(blk, 16, 1)
    return blk * cos + jnp.where(low_half, up, down) * sin_signed


def _proj_kernel(*refs, rope, emit_f32):
    it = iter(refs)
    x_ref, shift_ref, scale_ref, gain_ref, w_ref, hg_ref, ones_ref = (next(it) for _ in range(7))
    if rope:
        cos_ref, sin_ref = next(it), next(it)
    q_ref, g_ref, ka_ref, vta_ref, kb_ref, vtb_ref = (next(it) for _ in range(6))
    if emit_f32:
        ka32_ref, va32_ref, kb32_ref, vb32_ref = (next(it) for _ in range(4))

    x = x_ref[0]
    ms = jnp.mean(x * x, axis=-1, keepdims=True)
    h = x * lax.rsqrt(ms + EPS) * gain_ref[...]
    h = h * (1.0 + scale_ref[0]) + shift_ref[0]
    hb = h.astype(BF16)

    lane = lax.broadcasted_iota(jnp.int32, (1, LANES), 1)
    if rope:
        cos = cos_ref[...]
        sin_signed = sin_ref[...]
        low_half = (lane % 32) < 16

    ones256 = ones_ref[...]

    def seg(off, width):
        return _dot(hb, w_ref[:, off:off + width])

    def normed_chunks(p, gains):
        out = []
        for c0 in range(0, p.shape[1], MXU_DIM):
            y = _head_rms(p[:, c0:c0 + MXU_DIM], ones256)
            for c1 in range(0, MXU_DIM, LANES):
                yc = y[:, c1:c1 + LANES] * gains[(c0 + c1) // LANES]
                if rope:
                    yc = _rope(yc, cos, sin_signed, low_half)
                out.append(yc)
        return out

    def store_q(chunks, head0):
        for hh in range(2 * len(chunks)):
            kv = hh // GROUP
            c = chunks[hh // 2]
            if hh % 2 != kv:
                c = pltpu.roll(c, HEAD_DIM, 1)
            keep = (lane < HEAD_DIM) if kv == 0 else (lane >= HEAD_DIM)
            q_ref[0, head0 + hh] = jnp.where(keep, c, 0.0).astype(BF16)

    def silu(v):
        return v * jax.nn.sigmoid(v)

    p_qa = seg(_OFF_QA, WIDTH_A)
    p_qb = seg(_OFF_QB, WIDTH_B)
    qa = normed_chunks(p_qa, [hg_ref[0:1, :]] * (WIDTH_A // LANES))
    p_kva = seg(_OFF_KA, 2 * KV_W)
    store_q(qa, 0)
    qb = normed_chunks(p_qb, [hg_ref[2:3, :]] * (WIDTH_B // LANES))
    p_kvb = seg(_OFF_KB, 2 * KV_W)
    store_q(qb, N_HEADS_A)
    k_both = jnp.concatenate([p_kva[:, 0:KV_W], p_kvb[:, 0:KV_W]], axis=1)
    ka, kb = normed_chunks(k_both, [hg_ref[1:2, :], hg_ref[3:4, :]])
    p_ga = seg(_OFF_GA, WIDTH_A)
    p_gb = seg(_OFF_GB, WIDTH_B)
    ka_ref[0] = ka.astype(BF16)
    kb_ref[0] = kb.astype(BF16)
    va = p_kva[:, KV_W:2 * KV_W]
    vb = p_kvb[:, KV_W:2 * KV_W]
    vta_ref[0] = va.T.astype(BF16)
    vtb_ref[0] = vb.T.astype(BF16)
    g_ref[0, :, 0:WIDTH_A] = silu(p_ga).astype(BF16)
    g_ref[0, :, WIDTH_A:MIX_WIDTH] = silu(p_gb).astype(BF16)
    if emit_f32:
        for ref, val in ((ka32_ref, ka), (va32_ref, va), (kb32_ref, kb), (vb32_ref, vb)):
            ref[...] = val.reshape(ref.shape)


def _project(x, shift, scale, gain, w_in_bf, head_gains, ones_bd, rope_tables, emit_f32):
    b, n, _ = x.shape
    tm = min(PROJ_TILE, n)
    rope = rope_tables is not None
    per_batch = shift.shape[0] != 1
    mod_map = (lambda i, j: (i, 0, 0)) if per_batch else (lambda i, j: (0, 0, 0))
    in_specs = [
        pl.BlockSpec((1, tm, D_MODEL), lambda i, j: (i, j, 0)),
        pl.BlockSpec((1, 1, D_MODEL), mod_map),
        pl.BlockSpec((1, 1, D_MODEL), mod_map),
        pl.BlockSpec((1, D_MODEL), lambda i, j: (0, 0)),
        pl.BlockSpec((D_MODEL, IN_WIDTH), lambda i, j: (0, 0)),
        pl.BlockSpec((4, LANES), lambda i, j: (0, 0)),
        pl.BlockSpec((MXU_DIM, MXU_DIM), lambda i, j: (0, 0)),
    ]
    args = [x, shift, scale, gain, w_in_bf, head_gains, ones_bd]
    if rope:
        in_specs += [pl.BlockSpec((tm, LANES), lambda i, j: (j, 0))] * 2
        args += list(rope_tables)
    k_spec = pl.BlockSpec((1, tm, KV_W), lambda i, j: (i, j, 0))
    vt_spec = pl.BlockSpec((1, KV_W, tm), lambda i, j: (i, 0, j))
    out_specs = [
        pl.BlockSpec((1, N_HEADS, tm, LANES), lambda i, j: (i, 0, j, 0)),
        pl.BlockSpec((1, tm, MIX_WIDTH), lambda i, j: (i, j, 0)),
        k_spec, vt_spec, k_spec, vt_spec,
    ]
    k_shape = jax.ShapeDtypeStruct((b, n, KV_W), BF16)
    vt_shape = jax.ShapeDtypeStruct((b, KV_W, n), BF16)
    out_shape = [
        jax.ShapeDtypeStruct((b, N_HEADS, n, LANES), BF16),
        jax.ShapeDtypeStruct((b, n, MIX_WIDTH), BF16),
        k_shape, vt_shape, k_shape, vt_shape,
    ]
    if emit_f32:
        req = emit_f32
        out_specs += [pl.BlockSpec((tm // req, req, KV_W),
                                   lambda i, j: (i * (n // tm) + j, 0, 0))] * 4
        out_shape += [jax.ShapeDtypeStruct((b * n // req, req, KV_W), F32)] * 4
    return pl.pallas_call(
        functools.partial(_proj_kernel, rope=rope, emit_f32=bool(emit_f32)),
        grid=(b, n // tm),
        in_specs=in_specs,
        out_specs=out_specs,
        out_shape=out_shape,
        compiler_params=pltpu.CompilerParams(
            dimension_semantics=("arbitrary", "arbitrary"), vmem_limit_bytes=VMEM_LIMIT_BYTES),
        name="project_rope" if rope else "project_ctx",
    )(*args)


def _attn_kernel(*refs, n_ctx, windowed, n_lat, tq, gb, sub, own_keys, exp_lead):
    it = iter(refs)
    q_ref, g_ref, x_ref, gate_ref, ka_ref, vta_ref, kb_ref, vtb_ref = (next(it) for _ in range(8))
    if n_ctx:
        cka_ref, cva_ref, ckb_ref, cvb_ref = (next(it) for _ in range(4))
    sink_ref, wout_ref, out_ref = (next(it) for _ in range(3))
    o_scr = next(it)

    width = gb * tq

    if n_ctx:
        ctx_k = {"a": cka_ref[0].astype(BF16), "b": ckb_ref[0].astype(BF16)}
        ctx_vt = {"a": cva_ref[0].T.astype(BF16), "b": cvb_ref[0].T.astype(BF16)}

    band = tq + 2 * WINDOW

    def band_of(u):
        t = pl.program_id(1) * sub + u
        start = jnp.clip(t * tq - WINDOW, 0, n_lat - band)
        start = pl.multiple_of(start, WINDOW)
        kpos = start + lax.broadcasted_iota(jnp.int32, (band, tq), 0)
        qpos = t * tq + lax.broadcasted_iota(jnp.int32, (band, tq), 1)
        bias = jnp.where(jnp.abs(kpos - qpos) <= WINDOW, 0.0, NEG_INF).astype(F32)
        return start, jnp.concatenate([bias] * gb, axis=1)

    ck = KEY_CHUNK

    def spans(total):
        return [(r, min(ck, total - r)) for r in range(0, total, ck)]

    def chunks(u, mixer, kv):
        rows = slice(kv * HEAD_DIM, (kv + 1) * HEAD_DIM)
        k0 = u * n_lat if own_keys else 0
        out = []
        for r, n in spans(n_ctx):
            out.append((n,
                        functools.partial(lambda r, n: ctx_k[mixer][r:r + n, :], r, n),
                        functools.partial(lambda r, n: ctx_vt[mixer][rows, r:r + n], r, n), None))
        if mixer == "a":
            for r, n in spans(n_lat):
                r += k0
                out.append((n,
                            functools.partial(lambda r, n: ka_ref[0, r:r + n, :], r, n),
                            functools.partial(lambda r, n: vta_ref[0, rows, r:r + n], r, n), None))
        elif windowed:
            start, bias = band_of(u)
            for r, n in spans(band):
                out.append((n,
                            functools.partial(
                                lambda r, n: kb_ref[0, pl.ds(k0 + start + r, n), :], r, n),
                            functools.partial(
                                lambda r, n: vtb_ref[0, rows, pl.ds(k0 + start + r, n)], r, n),
                            bias[r:r + n, :]))
        else:
            for r, n in spans(n_lat):
                r += k0
                out.append((n,
                            functools.partial(lambda r, n: kb_ref[0, r:r + n, :], r, n),
                            functools.partial(lambda r, n: vtb_ref[0, rows, r:r + n], r, n), None))
        return out

    def sink_row(h0):
        return jnp.concatenate(
            [jnp.full((1, tq), sink_ref[h0 - N_HEADS_A + j] * LOG2E, F32) for j in range(gb)],
            axis=1)

    tasks = ([("a", h0) for h0 in range(0, N_HEADS_A, gb)]
             + [("b", N_HEADS_A + h0) for h0 in range(0, N_HEADS_B, gb)])
    items = []
    for u in range(sub):
        for ti, (mixer, h0) in enumerate(tasks):
            todo = chunks(u, mixer, (h0 % N_HEADS_A) // GROUP)
            for idx, chunk in enumerate(todo):
                last = idx == len(todo) - 1
                items.append((u, mixer, h0, chunk, idx == 0, last, last and ti == len(tasks) - 1))

    scores = {}
    running = {}
    tokens = []

    def emit_scores(k):
        u, mixer, h0, (_, load_k, _, kbias), _, _, _ = items[k]
        qg = q_ref[0, h0:h0 + gb, u * tq:(u + 1) * tq, :].reshape(width, LANES)
        s = _dot_t(load_k(), qg)
        scores[k] = s if kbias is None else s + kbias

    def emit_merge(u):
        toks = slice(u * tq, (u + 1) * tq)
        o = o_scr[u].T
        gated = (o * g_ref[0, toks, :].astype(F32)).astype(BF16)
        y = _dot(gated, wout_ref[...])
        out_ref[0, toks, :] = x_ref[0, toks, :] + gate_ref[0] * y

    def emit_softmax(k):
        u, mixer, h0, (nk, _, load_vt, _), first, last, tile_done = items[k]
        s = scores.pop(k)
        m_new = jnp.max(s, axis=0, keepdims=True)
        if first:
            if mixer == "b":
                m_new = jnp.maximum(m_new, sink_row(h0))
        else:
            m_old, o_old = running.pop((u, h0))
            m_new = jnp.maximum(m_old, m_new)
        if k >= exp_lead:
            m_new = _order_after(m_new, tokens[k - exp_lead])
        p = jnp.exp2(s - m_new)
        vt_ones = jnp.concatenate([load_vt(), jnp.ones((BF16_ROWS, nk), BF16)], axis=0)
        o = _dot(vt_ones, p.astype(BF16))
        if not first:
            o = o_old * jnp.exp2(m_old - m_new) + o
        tokens.append(o[0:8, 0:LANES])
        if not last:
            running[(u, h0)] = (m_new, o)
            return
        l = o[HEAD_DIM:HEAD_DIM + 1, :]
        if mixer == "b":
            l = l + jnp.exp2(sink_row(h0) - m_new)
        o = o[0:HEAD_DIM, :] * (1.0 / l)
        for j in range(gb):
            o_scr[u, (h0 + j) * HEAD_DIM:(h0 + j + 1) * HEAD_DIM, :] = o[:, j * tq:(j + 1) * tq]
        if tile_done:
            emit_merge(u)

    for k in range(len(items) + FILL_AHEAD):
        if k < len(items):
            emit_scores(k)
        if k >= FILL_AHEAD:
            emit_softmax(k - FILL_AHEAD)


def _attend(q, g, x, gate, ka, vta, kb, vtb, ctx, sink, w_out_bf, windowed, gb, sub, exp_lead,
            folded=1):
    b, n_all, _ = x.shape
    n = n_all // folded
    tq = min(TOKEN_TILE, n)
    nt = n // tq
    n_ctx = 0 if ctx is None else ctx[0].shape[1]
    own_keys = folded > 1
    if own_keys:
        assert b == 1 and nt == 1 and folded % sub == 0 and gate.shape[0] == 1
        grid = (folded // sub, 1)
        row = lambda i, j: 0
        tile = lambda i, j: i
        n_kblock = sub * n
        kblock = lambda i, j: i
    else:
        assert nt % sub == 0
        grid = (b, nt // sub)
        row = lambda i, j: i
        tile = lambda i, j: j
        n_kblock = n
        kblock = lambda i, j: 0
    per_batch = gate.shape[0] != 1
    gate_map = (lambda i, j: (i, 0, 0)) if per_batch else (lambda i, j: (0, 0, 0))
    tok_map = lambda i, j: (row(i, j), tile(i, j), 0)
    k_spec = pl.BlockSpec((1, n_kblock, KV_W), lambda i, j: (row(i, j), kblock(i, j), 0))
    vt_spec = pl.BlockSpec((1, KV_W, n_kblock), lambda i, j: (row(i, j), 0, kblock(i, j)))
    in_specs = [
        pl.BlockSpec((1, N_HEADS, sub * tq, LANES), lambda i, j: (row(i, j), 0, tile(i, j), 0)),
        pl.BlockSpec((1, sub * tq, MIX_WIDTH), tok_map),
        pl.BlockSpec((1, sub * tq, D_MODEL), tok_map),
        pl.BlockSpec((1, 1, D_MODEL), gate_map),
        k_spec, vt_spec, k_spec, vt_spec,
    ]
    args = [q, g, x, gate, ka, vta, kb, vtb]
    if n_ctx:
        in_specs += [pl.BlockSpec((1, n_ctx, KV_W), lambda i, j: (i, 0, 0))] * 4
        args += list(ctx)
    in_specs += [
        pl.BlockSpec(memory_space=pltpu.SMEM),
        pl.BlockSpec((MIX_WIDTH, D_MODEL), lambda i, j: (0, 0)),
    ]
    args += [sink, w_out_bf]
    return pl.pallas_call(
        functools.partial(_attn_kernel, n_ctx=n_ctx, windowed=windowed, n_lat=n, tq=tq, gb=gb,
                          sub=sub, own_keys=own_keys, exp_lead=exp_lead),
        grid=grid,
        in_specs=in_specs,
        out_specs=pl.BlockSpec((1, sub * tq, D_MODEL), tok_map),
        out_shape=jax.ShapeDtypeStruct((b, n_all, D_MODEL), F32),
        scratch_shapes=[
            pltpu.VMEM((sub, MIX_WIDTH, tq), F32),
        ],
        compiler_params=pltpu.CompilerParams(
            dimension_semantics=("arbitrary", "arbitrary"), vmem_limit_bytes=VMEM_LIMIT_BYTES),
        name="attend_latent" if windowed else "attend_ctx",
    )(*args)


def _ctx_kernel(*refs, tq, sub, exp_lead):
    (x_ref, shift_ref, scale_ref, gate_ref, gain_ref, w_ref, hg_ref, ones_ref, sink_ref, wout_ref,
     out_ref, ka32_ref, va32_ref, kb32_ref, vb32_ref,
     q_scr, g_scr, ka_scr, vta_scr, kb_scr, vtb_scr, o_scr) = refs
    _proj_kernel(x_ref, shift_ref, scale_ref, gain_ref, w_ref, hg_ref, ones_ref,
                 q_scr, g_scr, ka_scr, vta_scr, kb_scr, vtb_scr,
                 ka32_ref, va32_ref, kb32_ref, vb32_ref, rope=False, emit_f32=True)
    _attn_kernel(q_scr, g_scr, x_ref, gate_ref, ka_scr, vta_scr, kb_scr, vtb_scr,
                 sink_ref, wout_ref, out_ref, o_scr,
                 n_ctx=0, windowed=False, n_lat=tq, tq=tq, gb=GROUP, sub=sub, own_keys=True,
                 exp_lead=exp_lead)


def _context_pass(x, shift, scale, gate, gain, w_in_bf, head_gains, ones_bd, sink, w_out_bf,
                  sub, exp_lead):
    requests, n, _ = x.shape
    tm = sub * n
    x_flat = x.reshape(1, requests * n, D_MODEL)
    const = lambda *idx: (lambda i: idx)
    tok_spec = pl.BlockSpec((1, tm, D_MODEL), lambda i: (0, i, 0))
    kv32_spec = pl.BlockSpec((sub, n, KV_W), lambda i: (i, 0, 0))
    outs = pl.pallas_call(
        functools.partial(_ctx_kernel, tq=n, sub=sub, exp_lead=exp_lead),
        grid=(requests // sub,),
        in_specs=[
            tok_spec,
            pl.BlockSpec((1, 1, D_MODEL), const(0, 0, 0)),
            pl.BlockSpec((1, 1, D_MODEL), const(0, 0, 0)),
            pl.BlockSpec((1, 1, D_MODEL), const(0, 0, 0)),
            pl.BlockSpec((1, D_MODEL), const(0, 0)),
            pl.BlockSpec((D_MODEL, IN_WIDTH), const(0, 0)),
            pl.BlockSpec((4, LANES), const(0, 0)),
            pl.BlockSpec((MXU_DIM, MXU_DIM), const(0, 0)),
            pl.BlockSpec(memory_space=pltpu.SMEM),
            pl.BlockSpec((MIX_WIDTH, D_MODEL), const(0, 0)),
        ],
        out_specs=[tok_spec] + [kv32_spec] * 4,
        out_shape=[jax.ShapeDtypeStruct((1, requests * n, D_MODEL), F32)]
        + [jax.ShapeDtypeStruct((requests, n, KV_W), F32)] * 4,
        scratch_shapes=[
            pltpu.VMEM((1, N_HEADS, tm, LANES), BF16),
            pltpu.VMEM((1, tm, MIX_WIDTH), BF16),
            pltpu.VMEM((1, tm, KV_W), BF16),
            pltpu.VMEM((1, KV_W, tm), BF16),
            pltpu.VMEM((1, tm, KV_W), BF16),
            pltpu.VMEM((1, KV_W, tm), BF16),
            pltpu.VMEM((sub, MIX_WIDTH, n), F32),
        ],
        compiler_params=pltpu.CompilerParams(
            dimension_semantics=("arbitrary",), vmem_limit_bytes=VMEM_LIMIT_BYTES),
        name="context_pass",
    )(x_flat, shift, scale, gate, gain, w_in_bf, head_gains, ones_bd, sink, w_out_bf)
    return (outs[0].reshape(requests, n, D_MODEL),) + tuple(outs[1:])


def _rope_tables(n_tokens):
    rows = n_tokens // GRID_W
    row = jnp.repeat(jnp.arange(rows, dtype=F32), GRID_W)
    col = jnp.tile(jnp.arange(GRID_W, dtype=F32), rows)
    n_freq = HEAD_DIM // 4
    inv = ROPE_THETA ** (-jnp.arange(n_freq, dtype=F32) / n_freq)
    ar = row[:, None] * inv[None, :]
    ac = col[:, None] * inv[None, :]
    ang = jnp.concatenate([ar, ar, ac, ac], axis=-1)
    sign = jnp.asarray(np.tile(np.repeat([-1.0, 1.0], 16), HEAD_DIM // 32), F32)
    cos = jnp.tile(jnp.cos(ang), (1, LANES // HEAD_DIM))
    sin_signed = jnp.tile(jnp.sin(ang) * sign[None, :], (1, LANES // HEAD_DIM))
    return cos, sin_signed


def _ones_blockdiag():
    idx = np.arange(MXU_DIM) // HEAD_DIM
    return jnp.asarray(idx[:, None] == idx[None, :], BF16)


def kernel(x_prompt, x_sample, cache_k_a, cache_v_a, cache_k_b, cache_v_b, c, c_ctx,
           w_mod, b_mod, norm_gain, w_in, qn_a, kn_a, qn_b, kn_b, sink_b, w_out):
    depth = w_in.shape[0]
    batch, seq, _ = x_prompt.shape
    dec_batch, dec_seq, _ = x_sample.shape
    past = cache_k_a.shape[2]

    rope_tables = _rope_tables(dec_seq)
    ones_bd = _ones_blockdiag()
    n_cond = 1 + dec_batch
    cond_rows = -(-n_cond // 8) * 8
    cond = jnp.concatenate(
        [c_ctx[None, :], c, jnp.zeros((cond_rows - n_cond, D_MODEL), F32)], axis=0)

    xp, xs = x_prompt, x_sample
    new_kv = [[], [], [], []]
    tile2 = lambda v: jnp.tile(v, LANES // HEAD_DIM)
    for l in range(depth):
        w_in_bf = w_in[l].astype(BF16)
        w_out_bf = w_out[l].astype(BF16)
        q_scale = HEAD_DIM ** -0.5 * LOG2E
        head_gains = jnp.stack([tile2(qn_a[l]) * q_scale, tile2(kn_a[l]),
                                tile2(qn_b[l]) * q_scale, tile2(kn_b[l])])
        gain = norm_gain[l].reshape(1, D_MODEL)
        sink = sink_b[l].astype(F32)

        m = _modulation(cond, w_mod[l], b_mod[l])
        shift, scale, gate = (m[:, i * D_MODEL:(i + 1) * D_MODEL] for i in range(3))

        sel = lambda v: v[0:1].reshape(1, 1, D_MODEL)
        xp, ka32, va32, kb32, vb32 = _context_pass(
            xp, sel(shift), sel(scale), sel(gate), gain, w_in_bf, head_gains, ones_bd, sink,
            w_out_bf, CTX_TILES_PER_STEP, 3)
        for acc, v in zip(new_kv, (ka32, va32, kb32, vb32)):
            acc.append(v.reshape(batch, seq, N_KV_A, HEAD_DIM))

        sel = lambda v: v[1:n_cond].reshape(dec_batch, 1, D_MODEL)
        q, g, ka, vta, kb, vtb = _project(
            xs, sel(shift), sel(scale), gain, w_in_bf, head_gains, ones_bd, rope_tables, None)
        ctx = tuple(cache[:, l].reshape(dec_batch, past, KV_W)
                    for cache in (cache_k_a, cache_v_a, cache_k_b, cache_v_b))
        xs = _attend(q, g, xs, sel(gate), ka, vta, kb, vtb, ctx, sink, w_out_bf, True, 2,
                     LATENT_TILES_PER_STEP, 2)

    return (xp, xs) + tuple(jnp.stack(v, axis=1) for v in new_kv)
```

```python
import functools

import numpy as np
import jax
import jax.numpy as jnp
from jax import lax
from jax.experimental import pallas as pl
from jax.experimental.pallas import tpu as pltpu

F32 = jnp.float32
BF16 = jnp.bfloat16

D_MODEL = 1024
HEAD_DIM = 64
N_HEADS_A = 8
N_KV_A = 2
N_HEADS_B = 8
N_KV_B = 2
N_HEADS = N_HEADS_A + N_HEADS_B
GROUP = N_HEADS_A // N_KV_A
WIDTH_A = N_HEADS_A * HEAD_DIM
WIDTH_B = N_HEADS_B * HEAD_DIM
MIX_WIDTH = WIDTH_A + WIDTH_B
KV_W = N_KV_A * HEAD_DIM
IN_WIDTH = 2 * (2 * WIDTH_A + 2 * KV_W)
GRID_W = 64
WINDOW = 128
ROPE_THETA = 10000.0
EPS = 1e-6
NEG_INF = -1e30
LOG2E = 1.4426950408889634

LANES = 128
MXU_DIM = 256
BF16_ROWS = 16
VMEM_LIMIT_BYTES = 56 * 1024 * 1024

_OFF_QA = 0
_OFF_KA = _OFF_QA + WIDTH_A
_OFF_VA = _OFF_KA + KV_W
_OFF_GA = _OFF_VA + KV_W
_OFF_QB = _OFF_GA + WIDTH_A
_OFF_KB = _OFF_QB + WIDTH_B
_OFF_VB = _OFF_KB + KV_W
_OFF_GB = _OFF_VB + KV_W

TOKEN_TILE = 256
PROJ_TILE = 1024
KEY_CHUNK = 256
CTX_TILES_PER_STEP = 4
LATENT_TILES_PER_STEP = 4
FILL_AHEAD = 3


def _dot(a, b):
    return jnp.dot(a, b, preferred_element_type=F32)


def _dot_t(a, b):
    return lax.dot_general(a, b, (((1,), (1,)), ((), ())), preferred_element_type=F32)


def _order_after(x, token):
    zero = (pltpu.bitcast(token, jnp.uint32) >> 16) >> 16
    bits = pltpu.bitcast(x, jnp.uint32)
    zero = jnp.concatenate([zero[0:1, :]] * (bits.shape[1] // LANES), axis=1)
    return pltpu.bitcast(bits | jnp.broadcast_to(zero, bits.shape), x.dtype)


def _mod_kernel(cond_ref, w_ref, b_ref, out_ref):
    c = cond_ref[...]
    s = c * jax.nn.sigmoid(c)
    out_ref[...] = _dot(s.astype(BF16), w_ref[...].astype(BF16)) + b_ref[...]


def _modulation(cond, w_mod, b_mod):
    rows = cond.shape[0]
    n_out = w_mod.shape[1]
    bn = 768
    return pl.pallas_call(
        _mod_kernel,
        grid=(n_out // bn,),
        in_specs=[
            pl.BlockSpec((rows, D_MODEL), lambda j: (0, 0)),
            pl.BlockSpec((D_MODEL, bn), lambda j: (0, j)),
            pl.BlockSpec((1, bn), lambda j: (0, j)),
        ],
        out_specs=pl.BlockSpec((rows, bn), lambda j: (0, j)),
        out_shape=jax.ShapeDtypeStruct((rows, n_out), F32),
        compiler_params=pltpu.CompilerParams(
            dimension_semantics=("arbitrary",), vmem_limit_bytes=VMEM_LIMIT_BYTES),
        name="modulation",
    )(cond, w_mod, b_mod.reshape(1, n_out))


def _head_rms(blk, ones_blockdiag):
    ss = _dot((blk * blk).astype(BF16), ones_blockdiag)
    return blk * lax.rsqrt(ss * (1.0 / HEAD_DIM) + EPS)


def _rope(blk, cos, sin_signed, low_half):
    up = pltpu.roll(blk, LANES - 16, 1)
    down = pltpu.roll(blk, 16, 1)
    return blk * cos + jnp.where(low_half, up, down) * sin_signed


def _proj_kernel(*refs, rope, emit_f32):
    it = iter(refs)
    x_ref, shift_ref, scale_ref, gain_ref, w_ref, hg_ref, ones_ref = (next(it) for _ in range(7))
    if rope:
        cos_ref, sin_ref = next(it), next(it)
    q_ref, g_ref, ka_ref, vta_ref, kb_ref, vtb_ref = (next(it) for _ in range(6))
    if emit_f32:
        ka32_ref, va32_ref, kb32_ref, vb32_ref = (next(it) for _ in range(4))

    x = x_ref[0]
    ms = jnp.mean(x * x, axis=-1, keepdims=True)
    h = x * lax.rsqrt(ms + EPS) * gain_ref[...]
    h = h * (1.0 + scale_ref[0]) + shift_ref[0]
    hb = h.astype(BF16)

    lane = lax.broadcasted_iota(jnp.int32, (1, LANES), 1)
    if rope:
        cos = cos_ref[...]
        sin_signed = sin_ref[...]
        low_half = (lane % 32) < 16

    ones256 = ones_ref[...]

    def seg(off, width):
        return _dot(hb, w_ref[:, off:off + width])

    def normed_chunks(p, gains):
        out = []
        for c0 in range(0, p.shape[1], MXU_DIM):
            y = _head_rms(p[:, c0:c0 + MXU_DIM], ones256)
            for c1 in range(0, MXU_DIM, LANES):
                yc = y[:, c1:c1 + LANES] * gains[(c0 + c1) // LANES]
                if rope:
                    yc = _rope(yc, cos, sin_signed, low_half)
                out.append(yc)
        return out

    def store_q(chunks, head0):
        for hh in range(2 * len(chunks)):
            kv = hh // GROUP
            c = chunks[hh // 2]
            if hh % 2 != kv:
                c = pltpu.roll(c, HEAD_DIM, 1)
            keep = (lane < HEAD_DIM) if kv == 0 else (lane >= HEAD_DIM)
            q_ref[0, head0 + hh] = jnp.where(keep, c, 0.0).astype(BF16)

    def silu(v):
        return v * jax.nn.sigmoid(v)

    p_qa = seg(_OFF_QA, WIDTH_A)
    p_qb = seg(_OFF_QB, WIDTH_B)
    qa = normed_chunks(p_qa, [hg_ref[0:1, :]] * (WIDTH_A // LANES))
    p_kva = seg(_OFF_KA, 2 * KV_W)
    store_q(qa, 0)
    qb = normed_chunks(p_qb, [hg_ref[2:3, :]] * (WIDTH_B // LANES))
    p_kvb = seg(_OFF_KB, 2 * KV_W)
    store_q(qb, N_HEADS_A)
    k_both = jnp.concatenate([p_kva[:, 0:KV_W], p_kvb[:, 0:KV_W]], axis=1)
    ka, kb = normed_chunks(k_both, [hg_ref[1:2, :], hg_ref[3:4, :]])
    p_ga = seg(_OFF_GA, WIDTH_A)
    p_gb = seg(_OFF_GB, WIDTH_B)
    ka_ref[0] = ka.astype(BF16)
    kb_ref[0] = kb.astype(BF16)
    va = p_kva[:, KV_W:2 * KV_W]
    vb = p_kvb[:, KV_W:2 * KV_W]
    vta_ref[0] = va.T.astype(BF16)
    vtb_ref[0] = vb.T.astype(BF16)
    g_ref[0, :, 0:WIDTH_A] = silu(p_ga).astype(BF16)
    g_ref[0, :, WIDTH_A:MIX_WIDTH] = silu(p_gb).astype(BF16)
    if emit_f32:
        for ref, val in ((ka32_ref, ka), (va32_ref, va), (kb32_ref, kb), (vb32_ref, vb)):
            ref[...] = val.reshape(ref.shape)


def _project(x, shift, scale, gain, w_in_bf, head_gains, ones_bd, rope_tables, emit_f32):
    b, n, _ = x.shape
    tm = min(PROJ_TILE, n)
    rope = rope_tables is not None
    per_batch = shift.shape[0] != 1
    mod_map = (lambda i, j: (i, 0, 0)) if per_batch else (lambda i, j: (0, 0, 0))
    in_specs = [
        pl.BlockSpec((1, tm, D_MODEL), lambda i, j: (i, j, 0)),
        pl.BlockSpec((1, 1, D_MODEL), mod_map),
        pl.BlockSpec((1, 1, D_MODEL), mod_map),
        pl.BlockSpec((1, D_MODEL), lambda i, j: (0, 0)),
        pl.BlockSpec((D_MODEL, IN_WIDTH), lambda i, j: (0, 0)),
        pl.BlockSpec((4, LANES), lambda i, j: (0, 0)),
        pl.BlockSpec((MXU_DIM, MXU_DIM), lambda i, j: (0, 0)),
    ]
    args = [x, shift, scale, gain, w_in_bf, head_gains, ones_bd]
    if rope:
        in_specs += [pl.BlockSpec((tm, LANES), lambda i, j: (j, 0))] * 2
        args += list(rope_tables)
    k_spec = pl.BlockSpec((1, tm, KV_W), lambda i, j: (i, j, 0))
    vt_spec = pl.BlockSpec((1, KV_W, tm), lambda i, j: (i, 0, j))
    out_specs = [
        pl.BlockSpec((1, N_HEADS, tm, LANES), lambda i, j: (i, 0, j, 0)),
        pl.BlockSpec((1, tm, MIX_WIDTH), lambda i, j: (i, j, 0)),
        k_spec, vt_spec, k_spec, vt_spec,
    ]
    k_shape = jax.ShapeDtypeStruct((b, n, KV_W), BF16)
    vt_shape = jax.ShapeDtypeStruct((b, KV_W, n), BF16)
    out_shape = [
        jax.ShapeDtypeStruct((b, N_HEADS, n, LANES), BF16),
        jax.ShapeDtypeStruct((b, n, MIX_WIDTH), BF16),
        k_shape, vt_shape, k_shape, vt_shape,
    ]
    if emit_f32:
        req = emit_f32
        out_specs += [pl.BlockSpec((tm // req, req, KV_W),
                                   lambda i, j: (i * (n // tm) + j, 0, 0))] * 4
        out_shape += [jax.ShapeDtypeStruct((b * n // req, req, KV_W), F32)] * 4
    return pl.pallas_call(
        functools.partial(_proj_kernel, rope=rope, emit_f32=bool(emit_f32)),
        grid=(b, n // tm),
        in_specs=in_specs,
        out_specs=out_specs,
        out_shape=out_shape,
        compiler_params=pltpu.CompilerParams(
            dimension_semantics=("arbitrary", "arbitrary"), vmem_limit_bytes=VMEM_LIMIT_BYTES),
        name="project_rope" if rope else "project_ctx",
    )(*args)


def _attn_kernel(*refs, n_ctx, windowed, n_lat, tq, gb, sub, own_keys, exp_lead):
    it = iter(refs)
    q_ref, g_ref, x_ref, gate_ref, ka_ref, vta_ref, kb_ref, vtb_ref = (next(it) for _ in range(8))
    if n_ctx:
        cka_ref, cva_ref, ckb_ref, cvb_ref = (next(it) for _ in range(4))
    sink_ref, wout_ref, out_ref = (next(it) for _ in range(3))
    o_scr = next(it)

    width = gb * tq

    if n_ctx:
        ctx_k = {"a": cka_ref[0].astype(BF16), "b": ckb_ref[0].astype(BF16)}
        ctx_vt = {"a": cva_ref[0].T.astype(BF16), "b": cvb_ref[0].T.astype(BF16)}

    band = tq + 2 * WINDOW

    def band_of(u):
        t = pl.program_id(1) * sub + u
        start = jnp.clip(t * tq - WINDOW, 0, n_lat - band)
        start = pl.multiple_of(start, WINDOW)
        kpos = start + lax.broadcasted_iota(jnp.int32, (band, tq), 0)
        qpos = t * tq + lax.broadcasted_iota(jnp.int32, (band, tq), 1)
        bias = jnp.where(jnp.abs(kpos - qpos) <= WINDOW, 0.0, NEG_INF).astype(F32)
        return start, jnp.concatenate([bias] * gb, axis=1)

    ck = KEY_CHUNK

    def spans(total):
        return [(r, min(ck, total - r)) for r in range(0, total, ck)]

    def chunks(u, mixer, kv):
        rows = slice(kv * HEAD_DIM, (kv + 1) * HEAD_DIM)
        k0 = u * n_lat if own_keys else 0
        out = []
        for r, n in spans(n_ctx):
            out.append((n,
                        functools.partial(lambda r, n: ctx_k[mixer][r:r + n, :], r, n),
                        functools.partial(lambda r, n: ctx_vt[mixer][rows, r:r + n], r, n), None))
        if mixer == "a":
            for r, n in spans(n_lat):
                r += k0
                out.append((n,
                            functools.partial(lambda r, n: ka_ref[0, r:r + n, :], r, n),
                            functools.partial(lambda r, n: vta_ref[0, rows, r:r + n], r, n), None))
        elif windowed:
            start, bias = band_of(u)
            for r, n in spans(band):
                out.append((n,
                            functools.partial(
                                lambda r, n: kb_ref[0, pl.ds(k0 + start + r, n), :], r, n),
                            functools.partial(
                                lambda r, n: vtb_ref[0, rows, pl.ds(k0 + start + r, n)], r, n),
                            bias[r:r + n, :]))
        else:
            for r, n in spans(n_lat):
                r += k0
                out.append((n,
                            functools.partial(lambda r, n: kb_ref[0, r:r + n, :], r, n),
                            functools.partial(lambda r, n: vtb_ref[0, rows, r:r + n], r, n), None))
        return out

    def sink_row(h0):
        return jnp.concatenate(
            [jnp.full((1, tq), sink_ref[h0 - N_HEADS_A + j] * LOG2E, F32) for j in range(gb)],
            axis=1)

    tasks = ([("a", h0) for h0 in range(0, N_HEADS_A, gb)]
             + [("b", N_HEADS_A + h0) for h0 in range(0, N_HEADS_B, gb)])
    items = []
    for u in range(sub):
        for ti, (mixer, h0) in enumerate(tasks):
            todo = chunks(u, mixer, (h0 % N_HEADS_A) // GROUP)
            for idx, chunk in enumerate(todo):
                last = idx == len(todo) - 1
                items.append((u, mixer, h0, chunk, idx == 0, last, last and ti == len(tasks) - 1))

    scores = {}
    running = {}
    tokens = []

    def emit_scores(k):
        u, mixer, h0, (_, load_k, _, kbias), _, _, _ = items[k]
        qg = q_ref[0, h0:h0 + gb, u * tq:(u + 1) * tq, :].reshape(width, LANES)
        s = _dot_t(load_k(), qg)
        scores[k] = s if kbias is None else s + kbias

    def emit_merge(u):
        toks = slice(u * tq, (u + 1) * tq)
        o = o_scr[u].T
        gated = (o * g_ref[0, toks, :].astype(F32)).astype(BF16)
        y = _dot(gated, wout_ref[...])
        out_ref[0, toks, :] = x_ref[0, toks, :] + gate_ref[0] * y

    def emit_softmax(k):
        u, mixer, h0, (nk, _, load_vt, _), first, last, tile_done = items[k]
        s = scores.pop(k)
        m_new = jnp.max(s, axis=0, keepdims=True)
        if first:
            if mixer == "b":
                m_new = jnp.maximum(m_new, sink_row(h0))
        else:
            m_old, o_old = running.pop((u, h0))
            m_new = jnp.maximum(m_old, m_new)
        if k >= exp_lead:
            m_new = _order_after(m_new, tokens[k - exp_lead])
        p = jnp.exp2(s - m_new)
        vt_ones = jnp.concatenate([load_vt(), jnp.ones((BF16_ROWS, nk), BF16)], axis=0)
        o = _dot(vt_ones, p.astype(BF16))
        if not first:
            o = o_old * jnp.exp2(m_old - m_new) + o
        tokens.append(o[0:8, 0:LANES])
        if not last:
            running[(u, h0)] = (m_new, o)
            return
        l = o[HEAD_DIM:HEAD_DIM + 1, :]
        if mixer == "b":
            l = l + jnp.exp2(sink_row(h0) - m_new)
        o = o[0:HEAD_DIM, :] * (1.0 / l)
        for j in range(gb):
            o_scr[u, (h0 + j) * HEAD_DIM:(h0 + j + 1) * HEAD_DIM, :] = o[:, j * tq:(j + 1) * tq]
        if tile_done:
            emit_merge(u)

    for k in range(len(items) + FILL_AHEAD):
        if k < len(items):
            emit_scores(k)
        if k >= FILL_AHEAD:
            emit_softmax(k - FILL_AHEAD)


def _attend(q, g, x, gate, ka, vta, kb, vtb, ctx, sink, w_out_bf, windowed, gb, sub, exp_lead,
            folded=1):
    b, n_all, _ = x.shape
    n = n_all // folded
    tq = min(TOKEN_TILE, n)
    nt = n // tq
    n_ctx = 0 if ctx is None else ctx[0].shape[1]
    own_keys = folded > 1
    if own_keys:
        assert b == 1 and nt == 1 and folded % sub == 0 and gate.shape[0] == 1
        grid = (folded // sub, 1)
        row = lambda i, j: 0
        tile = lambda i, j: i
        n_kblock = sub * n
        kblock = lambda i, j: i
    else:
        assert nt % sub == 0
        grid = (b, nt // sub)
        row = lambda i, j: i
        tile = lambda i, j: j
        n_kblock = n
        kblock = lambda i, j: 0
    per_batch = gate.shape[0] != 1
    gate_map = (lambda i, j: (i, 0, 0)) if per_batch else (lambda i, j: (0, 0, 0))
    tok_map = lambda i, j: (row(i, j), tile(i, j), 0)
    k_spec = pl.BlockSpec((1, n_kblock, KV_W), lambda i, j: (row(i, j), kblock(i, j), 0))
    vt_spec = pl.BlockSpec((1, KV_W, n_kblock), lambda i, j: (row(i, j), 0, kblock(i, j)))
    in_specs = [
        pl.BlockSpec((1, N_HEADS, sub * tq, LANES), lambda i, j: (row(i, j), 0, tile(i, j), 0)),
        pl.BlockSpec((1, sub * tq, MIX_WIDTH), tok_map),
        pl.BlockSpec((1, sub * tq, D_MODEL), tok_map),
        pl.BlockSpec((1, 1, D_MODEL), gate_map),
        k_spec, vt_spec, k_spec, vt_spec,
    ]
    args = [q, g, x, gate, ka, vta, kb, vtb]
    if n_ctx:
        in_specs += [pl.BlockSpec((1, n_ctx, KV_W), lambda i, j: (i, 0, 0))] * 4
        args += list(ctx)
    in_specs += [
        pl.BlockSpec(memory_space=pltpu.SMEM),
        pl.BlockSpec((MIX_WIDTH, D_MODEL), lambda i, j: (0, 0)),
    ]
    args += [sink, w_out_bf]
    return pl.pallas_call(
        functools.partial(_attn_kernel, n_ctx=n_ctx, windowed=windowed, n_lat=n, tq=tq, gb=gb,
                          sub=sub, own_keys=own_keys, exp_lead=exp_lead),
        grid=grid,
        in_specs=in_specs,
        out_specs=pl.BlockSpec((1, sub * tq, D_MODEL), tok_map),
        out_shape=jax.ShapeDtypeStruct((b, n_all, D_MODEL), F32),
        scratch_shapes=[
            pltpu.VMEM((sub, MIX_WIDTH, tq), F32),
        ],
        compiler_params=pltpu.CompilerParams(
            dimension_semantics=("arbitrary", "arbitrary"), vmem_limit_bytes=VMEM_LIMIT_BYTES),
        name="attend_latent" if windowed else "attend_ctx",
    )(*args)


def _ctx_kernel(*refs, tq, sub, exp_lead):
    (x_ref, shift_ref, scale_ref, gate_ref, gain_ref, w_ref, hg_ref, ones_ref, sink_ref, wout_ref,
     out_ref, ka32_ref, va32_ref, kb32_ref, vb32_ref,
     q_scr, g_scr, ka_scr, vta_scr, kb_scr, vtb_scr, o_scr) = refs
    _proj_kernel(x_ref, shift_ref, scale_ref, gain_ref, w_ref, hg_ref, ones_ref,
                 q_scr, g_scr, ka_scr, vta_scr, kb_scr, vtb_scr,
                 ka32_ref, va32_ref, kb32_ref, vb32_ref, rope=False, emit_f32=True)
    _attn_kernel(q_scr, g_scr, x_ref, gate_ref, ka_scr, vta_scr, kb_scr, vtb_scr,
                 sink_ref, wout_ref, out_ref, o_scr,
                 n_ctx=0, windowed=False, n_lat=tq, tq=tq, gb=GROUP, sub=sub, own_keys=True,
                 exp_lead=exp_lead)


def _context_pass(x, shift, scale, gate, gain, w_in_bf, head_gains, ones_bd, sink, w_out_bf,
                  sub, exp_lead):
    requests, n, _ = x.shape
    tm = sub * n
    x_flat = x.reshape(1, requests * n, D_MODEL)
    const = lambda *idx: (lambda i: idx)
    tok_spec = pl.BlockSpec((1, tm, D_MODEL), lambda i: (0, i, 0))
    kv32_spec = pl.BlockSpec((sub, n, KV_W), lambda i: (i, 0, 0))
    outs = pl.pallas_call(
        functools.partial(_ctx_kernel, tq=n, sub=sub, exp_lead=exp_lead),
        grid=(requests // sub,),
        in_specs=[
            tok_spec,
            pl.BlockSpec((1, 1, D_MODEL), const(0, 0, 0)),
            pl.BlockSpec((1, 1, D_MODEL), const(0, 0, 0)),
            pl.BlockSpec((1, 1, D_MODEL), const(0, 0, 0)),
            pl.BlockSpec((1, D_MODEL), const(0, 0)),
            pl.BlockSpec((D_MODEL, IN_WIDTH), const(0, 0)),
            pl.BlockSpec((4, LANES), const(0, 0)),
            pl.BlockSpec((MXU_DIM, MXU_DIM), const(0, 0)),
            pl.BlockSpec(memory_space=pltpu.SMEM),
            pl.BlockSpec((MIX_WIDTH, D_MODEL), const(0, 0)),
        ],
        out_specs=[tok_spec] + [kv32_spec] * 4,
        out_shape=[jax.ShapeDtypeStruct((1, requests * n, D_MODEL), F32)]
        + [jax.ShapeDtypeStruct((requests, n, KV_W), F32)] * 4,
        scratch_shapes=[
            pltpu.VMEM((1, N_HEADS, tm, LANES), BF16),
            pltpu.VMEM((1, tm, MIX_WIDTH), BF16),
            pltpu.VMEM((1, tm, KV_W), BF16),
            pltpu.VMEM((1, KV_W, tm), BF16),
            pltpu.VMEM((1, tm, KV_W), BF16),
            pltpu.VMEM((1, KV_W, tm), BF16),
            pltpu.VMEM((sub, MIX_WIDTH, n), F32),
        ],
        compiler_params=pltpu.CompilerParams(
            dimension_semantics=("arbitrary",), vmem_limit_bytes=VMEM_LIMIT_BYTES),
        name="context_pass",
    )(x_flat, shift, scale, gate, gain, w_in_bf, head_gains, ones_bd, sink, w_out_bf)
    return (outs[0].reshape(requests, n, D_MODEL),) + tuple(outs[1:])


def _rope_tables(n_tokens):
    rows = n_tokens // GRID_W
    row = jnp.repeat(jnp.arange(rows, dtype=F32), GRID_W)
    col = jnp.tile(jnp.arange(GRID_W, dtype=F32), rows)
    n_freq = HEAD_DIM // 4
    inv = ROPE_THETA ** (-jnp.arange(n_freq, dtype=F32) / n_freq)
    ar = row[:, None] * inv[None, :]
    ac = col[:, None] * inv[None, :]
    ang = jnp.concatenate([ar, ar, ac, ac], axis=-1)
    sign = jnp.asarray(np.tile(np.repeat([-1.0, 1.0], 16), HEAD_DIM // 32), F32)
    cos = jnp.tile(jnp.cos(ang), (1, LANES // HEAD_DIM))
    sin_signed = jnp.tile(jnp.sin(ang) * sign[None, :], (1, LANES // HEAD_DIM))
    return cos, sin_signed


def _ones_blockdiag():
    idx = np.arange(MXU_DIM) // HEAD_DIM
    return jnp.asarray(idx[:, None] == idx[None, :], BF16)


def kernel(x_prompt, x_sample, cache_k_a, cache_v_a, cache_k_b, cache_v_b, c, c_ctx,
           w_mod, b_mod, norm_gain, w_in, qn_a, kn_a, qn_b, kn_b, sink_b, w_out):
    depth = w_in.shape[0]
    batch, seq, _ = x_prompt.shape
    dec_batch, dec_seq, _ = x_sample.shape
    past = cache_k_a.shape[2]

    rope_tables = _rope_tables(dec_seq)
    ones_bd = _ones_blockdiag()
    n_cond = 1 + dec_batch
    cond_rows = -(-n_cond // 8) * 8
    cond = jnp.concatenate(
        [c_ctx[None, :], c, jnp.zeros((cond_rows - n_cond, D_MODEL), F32)], axis=0)

    xp, xs = x_prompt, x_sample
    new_kv = [[], [], [], []]
    tile2 = lambda v: jnp.tile(v, LANES // HEAD_DIM)
    for l in range(depth):
        w_in_bf = w_in[l].astype(BF16)
        w_out_bf = w_out[l].astype(BF16)
        q_scale = HEAD_DIM ** -0.5 * LOG2E
        head_gains = jnp.stack([tile2(qn_a[l]) * q_scale, tile2(kn_a[l]),
                                tile2(qn_b[l]) * q_scale, tile2(kn_b[l])])
        gain = norm_gain[l].reshape(1, D_MODEL)
        sink = sink_b[l].astype(F32)

        m = _modulation(cond, w_mod[l], b_mod[l])
        shift, scale, gate = (m[:, i * D_MODEL:(i + 1) * D_MODEL] for i in range(3))

        sel = lambda v: v[0:1].reshape(1, 1, D_MODEL)
        xp, ka32, va32, kb32, vb32 = _context_pass(
            xp, sel(shift), sel(scale), sel(gate), gain, w_in_bf, head_gains, ones_bd, sink,
            w_out_bf, CTX_TILES_PER_STEP, 3)
        for acc, v in zip(new_kv, (ka32, va32, kb32, vb32)):
            acc.append(v.reshape(batch, seq, N_KV_A, HEAD_DIM))

        sel = lambda v: v[1:n_cond].reshape(dec_batch, 1, D_MODEL)
        q, g, ka, vta, kb, vtb = _project(
            xs, sel(shift), sel(scale), gain, w_in_bf, head_gains, ones_bd, rope_tables, None)
        ctx = tuple(cache[:, l].reshape(dec_batch, past, KV_W)
                    for cache in (cache_k_a, cache_v_a, cache_k_b, cache_v_b))
        xs = _attend(q, g, xs, sel(gate), ka, vta, kb, vtb, ctx, sink, w_out_bf, True, 2,
                     LATENT_TILES_PER_STEP, 2)

    return (xp, xs) + tuple(jnp.stack(v, axis=1) for v in new_kv)
```

```python
import functools

import numpy as np
import jax
import jax.numpy as jnp
from jax import lax
from jax.experimental import pallas as pl
from jax.experimental.pallas import tpu as pltpu

F32 = jnp.float32
BF16 = jnp.bfloat16

D_MODEL = 1024
HEAD_DIM = 64
N_HEADS_A = 8
N_KV_A = 2
N_HEADS_B = 8
N_HEADS = N_HEADS_A + N_HEADS_B
GROUP = N_HEADS_A // N_KV_A
WIDTH_A = N_HEADS_A * HEAD_DIM
WIDTH_B = N_HEADS_B * HEAD_DIM
MIX_WIDTH = WIDTH_A + WIDTH_B
KV_W = N_KV_A * HEAD_DIM
IN_WIDTH = 2 * (2 * WIDTH_A + 2 * KV_W)
GRID_W = 64
WINDOW = 128
ROPE_THETA = 10000.0
EPS = 1e-6
NEG_INF = -1e30
LOG2E = 1.4426950408889634

LANES = 128
MXU_DIM = 256
BF16_ROWS = 16
VMEM_LIMIT_BYTES = 56 * 1024 * 1024

_OFF_QA = 0
_OFF_KA = _OFF_QA + WIDTH_A
_OFF_VA = _OFF_KA + KV_W
_OFF_GA = _OFF_VA + KV_W
_OFF_QB = _OFF_GA + WIDTH_A
_OFF_KB = _OFF_QB + WIDTH_B
_OFF_VB = _OFF_KB + KV_W
_OFF_GB = _OFF_VB + KV_W

TOKEN_TILE = 256
PROJ_TILE = 1024
KEY_CHUNK = 256
CTX_TILES_PER_STEP = 4
LATENT_TILES_PER_STEP = 2
FILL_AHEAD = 3


def _dot(a, b):
    return jnp.dot(a, b, preferred_element_type=F32)


def _dot_t(a, b):
    return lax.dot_general(a, b, (((1,), (1,)), ((), ())), preferred_element_type=F32)


def _order_after(x, token):
    zero = (pltpu.bitcast(token, jnp.uint32) >> 16) >> 16
    bits = pltpu.bitcast(x, jnp.uint32)
    zero = jnp.concatenate([zero[0:1, :]] * (bits.shape[1] // LANES), axis=1)
    return pltpu.bitcast(bits | jnp.broadcast_to(zero, bits.shape), x.dtype)


def _mod_kernel(cond_ref, w_ref, b_ref, out_ref):
    c = cond_ref[...]
    s = c * jax.nn.sigmoid(c)
    out_ref[...] = _dot(s.astype(BF16), w_ref[...].astype(BF16)) + b_ref[...]


def _modulation(cond, w_mod, b_mod):
    rows = cond.shape[0]
    n_out = w_mod.shape[1]
    bn = 768
    return pl.pallas_call(
        _mod_kernel,
        grid=(n_out // bn,),
        in_specs=[
            pl.BlockSpec((rows, D_MODEL), lambda j: (0, 0)),
            pl.BlockSpec((D_MODEL, bn), lambda j: (0, j)),
            pl.BlockSpec((1, bn), lambda j: (0, j)),
        ],
        out_specs=pl.BlockSpec((rows, bn), lambda j: (0, j)),
        out_shape=jax.ShapeDtypeStruct((rows, n_out), F32),
        compiler_params=pltpu.CompilerParams(
            dimension_semantics=("arbitrary",), vmem_limit_bytes=VMEM_LIMIT_BYTES),
        name="modulation",
    )(cond, w_mod, b_mod.reshape(1, n_out))


def _head_rms(blk, ones_blockdiag):
    ss = _dot((blk * blk).astype(BF16), ones_blockdiag)
    return blk * lax.rsqrt(ss * (1.0 / HEAD_DIM) + EPS)


def _rope(blk, cos, sin_signed, low_half):
    up = pltpu.roll(blk, LANES - 16, 1)
    down = pltpu.roll(blk, 16, 1)
    return blk * cos + jnp.where(low_half, up, down) * sin_signed


def _proj_kernel(*refs, rope, emit_f32):
    it = iter(refs)
    x_ref, shift_ref, scale_ref, gain_ref, w_ref, hg_ref, ones_ref = (next(it) for _ in range(7))
    if rope:
        cos_ref, sin_ref = next(it), next(it)
    q_ref, g_ref, ka_ref, vta_ref, kb_ref, vtb_ref = (next(it) for _ in range(6))
    if emit_f32:
        ka32_ref, va32_ref, kb32_ref, vb32_ref = (next(it) for _ in range(4))

    x = x_ref[0]
    ms = jnp.mean(x * x, axis=-1, keepdims=True)
    h = x * lax.rsqrt(ms + EPS) * gain_ref[...]
    h = h * (1.0 + scale_ref[0]) + shift_ref[0]
    hb = h.astype(BF16)

    lane = lax.broadcasted_iota(jnp.int32, (1, LANES), 1)
    if rope:
        cos = cos_ref[...]
        sin_signed = sin_ref[...]
        low_half = (lane % 32) < 16

    ones256 = ones_ref[...]

    def seg(off, width):
        return _dot(hb, w_ref[:, off:off + width])

    def normed_chunks(p, gains):
        out = []
        for c0 in range(0, p.shape[1], MXU_DIM):
            y = _head_rms(p[:, c0:c0 + MXU_DIM], ones256)
            for c1 in range(0, MXU_DIM, LANES):
                yc = y[:, c1:c1 + LANES] * gains[(c0 + c1) // LANES]
                if rope:
                    yc = _rope(yc, cos, sin_signed, low_half)
                out.append(yc)
        return out

    def store_q(chunks, head0):
        for hh in range(2 * len(chunks)):
            kv = hh // GROUP
            c = chunks[hh // 2]
            if hh % 2 != kv:
                c = pltpu.roll(c, HEAD_DIM, 1)
            keep = (lane < HEAD_DIM) if kv == 0 else (lane >= HEAD_DIM)
            q_ref[0, head0 + hh] = jnp.where(keep, c, 0.0).astype(BF16)

    def silu(v):
        return v * jax.nn.sigmoid(v)

    p_qa = seg(_OFF_QA, WIDTH_A)
    p_qb = seg(_OFF_QB, WIDTH_B)
    qa = normed_chunks(p_qa, [hg_ref[0:1, :]] * (WIDTH_A // LANES))
    p_kva = seg(_OFF_KA, 2 * KV_W)
    store_q(qa, 0)
    qb = normed_chunks(p_qb, [hg_ref[2:3, :]] * (WIDTH_B // LANES))
    p_kvb = seg(_OFF_KB, 2 * KV_W)
    store_q(qb, N_HEADS_A)
    k_both = jnp.concatenate([p_kva[:, 0:KV_W], p_kvb[:, 0:KV_W]], axis=1)
    ka, kb = normed_chunks(k_both, [hg_ref[1:2, :], hg_ref[3:4, :]])
    p_ga = seg(_OFF_GA, WIDTH_A)
    p_gb = seg(_OFF_GB, WIDTH_B)
    ka_ref[0] = ka.astype(BF16)
    kb_ref[0] = kb.astype(BF16)
    va = p_kva[:, KV_W:2 * KV_W]
    vb = p_kvb[:, KV_W:2 * KV_W]
    vta_ref[0] = va.T.astype(BF16)
    vtb_ref[0] = vb.T.astype(BF16)
    g_ref[0, :, 0:WIDTH_A] = silu(p_ga).astype(BF16)
    g_ref[0, :, WIDTH_A:MIX_WIDTH] = silu(p_gb).astype(BF16)
    if emit_f32:
        for ref, val in ((ka32_ref, ka), (va32_ref, va), (kb32_ref, kb), (vb32_ref, vb)):
            ref[...] = val.reshape(ref.shape)


def _project(x, shift, scale, gain, w_in_bf, head_gains, ones_bd, rope_tables, emit_f32):
    b, n, _ = x.shape
    tm = min(PROJ_TILE, n)
    rope = rope_tables is not None
    per_batch = shift.shape[0] != 1
    mod_map = (lambda i, j: (i, 0, 0)) if per_batch else (lambda i, j: (0, 0, 0))
    in_specs = [
        pl.BlockSpec((1, tm, D_MODEL), lambda i, j: (i, j, 0)),
        pl.BlockSpec((1, 1, D_MODEL), mod_map),
        pl.BlockSpec((1, 1, D_MODEL), mod_map),
        pl.BlockSpec((1, D_MODEL), lambda i, j: (0, 0)),
        pl.BlockSpec((D_MODEL, IN_WIDTH), lambda i, j: (0, 0)),
        pl.BlockSpec((4, LANES), lambda i, j: (0, 0)),
        pl.BlockSpec((MXU_DIM, MXU_DIM), lambda i, j: (0, 0)),
    ]
    args = [x, shift, scale, gain, w_in_bf, head_gains, ones_bd]
    if rope:
        in_specs += [pl.BlockSpec((tm, LANES), lambda i, j: (j, 0))] * 2
        args += list(rope_tables)
    k_spec = pl.BlockSpec((1, tm, KV_W), lambda i, j: (i, j, 0))
    vt_spec = pl.BlockSpec((1, KV_W, tm), lambda i, j: (i, 0, j))
    out_specs = [
        pl.BlockSpec((1, N_HEADS, tm, LANES), lambda i, j: (i, 0, j, 0)),
        pl.BlockSpec((1, tm, MIX_WIDTH), lambda i, j: (i, j, 0)),
        k_spec, vt_spec, k_spec, vt_spec,
    ]
    k_shape = jax.ShapeDtypeStruct((b, n, KV_W), BF16)
    vt_shape = jax.ShapeDtypeStruct((b, KV_W, n), BF16)
    out_shape = [
        jax.ShapeDtypeStruct((b, N_HEADS, n, LANES), BF16),
        jax.ShapeDtypeStruct((b, n, MIX_WIDTH), BF16),
        k_shape, vt_shape, k_shape, vt_shape,
    ]
    if emit_f32:
        req = emit_f32
        out_specs += [pl.BlockSpec((tm // req, req, KV_W),
                                   lambda i, j: (i * (n // tm) + j, 0, 0))] * 4
        out_shape += [jax.ShapeDtypeStruct((b * n // req, req, KV_W), F32)] * 4
    return pl.pallas_call(
        functools.partial(_proj_kernel, rope=rope, emit_f32=bool(emit_f32)),
        grid=(b, n // tm),
        in_specs=in_specs,
        out_specs=out_specs,
        out_shape=out_shape,
        compiler_params=pltpu.CompilerParams(
            dimension_semantics=("arbitrary", "arbitrary"), vmem_limit_bytes=VMEM_LIMIT_BYTES),
        name="project_rope" if rope else "project_ctx",
    )(*args)


def _attn_kernel(*refs, n_ctx, windowed, n_lat, tq, gb, sub, own_keys, exp_lead):
    it = iter(refs)
    q_ref, g_ref, x_ref, gate_ref, ka_ref, vta_ref, kb_ref, vtb_ref = (next(it) for _ in range(8))
    if n_ctx:
        cka_ref, cva_ref, ckb_ref, cvb_ref = (next(it) for _ in range(4))
    sink_ref, wout_ref, out_ref = (next(it) for _ in range(3))
    o_scr = next(it)

    width = gb * tq

    if n_ctx:
        ctx_k = {"a": cka_ref[0].astype(BF16), "b": ckb_ref[0].astype(BF16)}
        ctx_vt = {"a": cva_ref[0].T.astype(BF16), "b": cvb_ref[0].T.astype(BF16)}

    band = tq + 2 * WINDOW

    def band_of(u):
        t = pl.program_id(1) * sub + u
        start = jnp.clip(t * tq - WINDOW, 0, n_lat - band)
        start = pl.multiple_of(start, WINDOW)
        kpos = start + lax.broadcasted_iota(jnp.int32, (band, tq), 0)
        qpos = t * tq + lax.broadcasted_iota(jnp.int32, (band, tq), 1)
        bias = jnp.where(jnp.abs(kpos - qpos) <= WINDOW, 0.0, NEG_INF).astype(F32)
        return start, jnp.concatenate([bias] * gb, axis=1)

    ck = KEY_CHUNK

    def spans(total):
        return [(r, min(ck, total - r)) for r in range(0, total, ck)]

    def chunks(u, mixer, kv):
        rows = slice(kv * HEAD_DIM, (kv + 1) * HEAD_DIM)
        k0 = u * n_lat if own_keys else 0
        out = []
        for r, n in spans(n_ctx):
            out.append((n,
                        functools.partial(lambda r, n: ctx_k[mixer][r:r + n, :], r, n),
                        functools.partial(lambda r, n: ctx_vt[mixer][rows, r:r + n], r, n), None))
        if mixer == "a":
            for r, n in spans(n_lat):
                r += k0
                out.append((n,
                            functools.partial(lambda r, n: ka_ref[0, r:r + n, :], r, n),
                            functools.partial(lambda r, n: vta_ref[0, rows, r:r + n], r, n), None))
        elif windowed:
            start, bias = band_of(u)
            for r, n in spans(band):
                out.append((n,
                            functools.partial(
                                lambda r, n: kb_ref[0, pl.ds(k0 + start + r, n), :], r, n),
                            functools.partial(
                                lambda r, n: vtb_ref[0, rows, pl.ds(k0 + start + r, n)], r, n),
                            bias[r:r + n, :]))
        else:
            for r, n in spans(n_lat):
                r += k0
                out.append((n,
                            functools.partial(lambda r, n: kb_ref[0, r:r + n, :], r, n),
                            functools.partial(lambda r, n: vtb_ref[0, rows, r:r + n], r, n), None))
        return out

    def sink_row(h0):
        return jnp.concatenate(
            [jnp.full((1, tq), sink_ref[h0 - N_HEADS_A + j] * LOG2E, F32) for j in range(gb)],
            axis=1)

    tasks = ([("a", h0) for h0 in range(0, N_HEADS_A, gb)]
             + [("b", N_HEADS_A + h0) for h0 in range(0, N_HEADS_B, gb)])
    items = []
    for u in range(sub):
        for ti, (mixer, h0) in enumerate(tasks):
            todo = chunks(u, mixer, (h0 % N_HEADS_A) // GROUP)
            for idx, chunk in enumerate(todo):
                last = idx == len(todo) - 1
                items.append((u, mixer, h0, chunk, idx == 0, last, last and ti == len(tasks) - 1))

    scores = {}
    running = {}
    tokens = []

    def emit_scores(k):
        u, mixer, h0, (_, load_k, _, kbias), _, _, _ = items[k]
        qg = q_ref[0, h0:h0 + gb, u * tq:(u + 1) * tq, :].reshape(width, LANES)
        s = _dot_t(load_k(), qg)
        scores[k] = s if kbias is None else s + kbias

    def emit_merge(u):
        toks = slice(u * tq, (u + 1) * tq)
        o = o_scr[u].T
        gated = (o * g_ref[0, toks, :].astype(F32)).astype(BF16)
        y = _dot(gated, wout_ref[...])
        out_ref[0, toks, :] = x_ref[0, toks, :] + gate_ref[0] * y

    def emit_softmax(k):
        u, mixer, h0, (nk, _, load_vt, _), first, last, tile_done = items[k]
        s = scores.pop(k)
        m_new = jnp.max(s, axis=0, keepdims=True)
        if first:
            if mixer == "b":
                m_new = jnp.maximum(m_new, sink_row(h0))
        else:
            m_old, o_old = running.pop((u, h0))
            m_new = jnp.maximum(m_old, m_new)
        if k >= exp_lead:
            m_new = _order_after(m_new, tokens[k - exp_lead])
        p = jnp.exp2(s - m_new)
        vt_ones = jnp.concatenate([load_vt(), jnp.ones((BF16_ROWS, nk), BF16)], axis=0)
        o = _dot(vt_ones, p.astype(BF16))
        if not first:
            o = o_old * jnp.exp2(m_old - m_new) + o
        tokens.append(o[0:8, 0:LANES])
        if not last:
            running[(u, h0)] = (m_new, o)
            return
        l = o[HEAD_DIM:HEAD_DIM + 1, :]
        if mixer == "b":
            l = l + jnp.exp2(sink_row(h0) - m_new)
        o = o[0:HEAD_DIM, :] * (1.0 / l)
        for j in range(gb):
            o_scr[u, (h0 + j) * HEAD_DIM:(h0 + j + 1) * HEAD_DIM, :] = o[:, j * tq:(j + 1) * tq]
        if tile_done:
            emit_merge(u)

    for k in range(len(items) + FILL_AHEAD):
        if k < len(items):
            emit_scores(k)
        if k >= FILL_AHEAD:
            emit_softmax(k - FILL_AHEAD)


def _attend(q, g, x, gate, ka, vta, kb, vtb, ctx, sink, w_out_bf, windowed, gb, sub, exp_lead,
            folded=1):
    b, n_all, _ = x.shape
    n = n_all // folded
    tq = min(TOKEN_TILE, n)
    nt = n // tq
    n_ctx = 0 if ctx is None else ctx[0].shape[1]
    own_keys = folded > 1
    if own_keys:
        assert b == 1 and nt == 1 and folded % sub == 0 and gate.shape[0] == 1
        grid = (folded // sub, 1)
        row = lambda i, j: 0
        tile = lambda i, j: i
        n_kblock = sub * n
        kblock = lambda i, j: i
    else:
        assert nt % sub == 0
        grid = (b, nt // sub)
        row = lambda i, j: i
        tile = lambda i, j: j
        n_kblock = n
        kblock = lambda i, j: 0
    per_batch = gate.shape[0] != 1
    gate_map = (lambda i, j: (i, 0, 0)) if per_batch else (lambda i, j: (0, 0, 0))
    tok_map = lambda i, j: (row(i, j), tile(i, j), 0)
    k_spec = pl.BlockSpec((1, n_kblock, KV_W), lambda i, j: (row(i, j), kblock(i, j), 0))
    vt_spec = pl.BlockSpec((1, KV_W, n_kblock), lambda i, j: (row(i, j), 0, kblock(i, j)))
    in_specs = [
        pl.BlockSpec((1, N_HEADS, sub * tq, LANES), lambda i, j: (row(i, j), 0, tile(i, j), 0)),
        pl.BlockSpec((1, sub * tq, MIX_WIDTH), tok_map),
        pl.BlockSpec((1, sub * tq, D_MODEL), tok_map),
        pl.BlockSpec((1, 1, D_MODEL), gate_map),
        k_spec, vt_spec, k_spec, vt_spec,
    ]
    args = [q, g, x, gate, ka, vta, kb, vtb]
    if n_ctx:
        in_specs += [pl.BlockSpec((1, n_ctx, KV_W), lambda i, j: (i, 0, 0))] * 4
        args += list(ctx)
    in_specs += [
        pl.BlockSpec(memory_space=pltpu.SMEM),
        pl.BlockSpec((MIX_WIDTH, D_MODEL), lambda i, j: (0, 0)),
    ]
    args += [sink, w_out_bf]
    return pl.pallas_call(
        functools.partial(_attn_kernel, n_ctx=n_ctx, windowed=windowed, n_lat=n, tq=tq, gb=gb,
                          sub=sub, own_keys=own_keys, exp_lead=exp_lead),
        grid=grid,
        in_specs=in_specs,
        out_specs=pl.BlockSpec((1, sub * tq, D_MODEL), tok_map),
        out_shape=jax.ShapeDtypeStruct((b, n_all, D_MODEL), F32),
        scratch_shapes=[
            pltpu.VMEM((sub, MIX_WIDTH, tq), F32),
        ],
        compiler_params=pltpu.CompilerParams(
            dimension_semantics=("arbitrary", "arbitrary"), vmem_limit_bytes=VMEM_LIMIT_BYTES),
        name="attend_latent" if windowed else "attend_ctx",
    )(*args)


def _ctx_kernel(*refs, tq, sub, exp_lead):
    (x_ref, shift_ref, scale_ref, gate_ref, gain_ref, w_ref, hg_ref, ones_ref, sink_ref, wout_ref,
     out_ref, ka32_ref, va32_ref, kb32_ref, vb32_ref,
     q_scr, g_scr, ka_scr, vta_scr, kb_scr, vtb_scr, o_scr) = refs
    _proj_kernel(x_ref, shift_ref, scale_ref, gain_ref, w_ref, hg_ref, ones_ref,
                 q_scr, g_scr, ka_scr, vta_scr, kb_scr, vtb_scr,
                 ka32_ref, va32_ref, kb32_ref, vb32_ref, rope=False, emit_f32=True)
    _attn_kernel(q_scr, g_scr, x_ref, gate_ref, ka_scr, vta_scr, kb_scr, vtb_scr,
                 sink_ref, wout_ref, out_ref, o_scr,
                 n_ctx=0, windowed=False, n_lat=tq, tq=tq, gb=GROUP, sub=sub, own_keys=True,
                 exp_lead=exp_lead)


def _context_pass(x, shift, scale, gate, gain, w_in_bf, head_gains, ones_bd, sink, w_out_bf,
                  sub, exp_lead):
    requests, n, _ = x.shape
    tm = sub * n
    x_flat = x.reshape(1, requests * n, D_MODEL)
    const = lambda *idx: (lambda i: idx)
    tok_spec = pl.BlockSpec((1, tm, D_MODEL), lambda i: (0, i, 0))
    kv32_spec = pl.BlockSpec((sub, n, KV_W), lambda i: (i, 0, 0))
    outs = pl.pallas_call(
        functools.partial(_ctx_kernel, tq=n, sub=sub, exp_lead=exp_lead),
        grid=(requests // sub,),
        in_specs=[
            tok_spec,
            pl.BlockSpec((1, 1, D_MODEL), const(0, 0, 0)),
            pl.BlockSpec((1, 1, D_MODEL), const(0, 0, 0)),
            pl.BlockSpec((1, 1, D_MODEL), const(0, 0, 0)),
            pl.BlockSpec((1, D_MODEL), const(0, 0)),
            pl.BlockSpec((D_MODEL, IN_WIDTH), const(0, 0)),
            pl.BlockSpec((4, LANES), const(0, 0)),
            pl.BlockSpec((MXU_DIM, MXU_DIM), const(0, 0)),
            pl.BlockSpec(memory_space=pltpu.SMEM),
            pl.BlockSpec((MIX_WIDTH, D_MODEL), const(0, 0)),
        ],
        out_specs=[tok_spec] + [kv32_spec] * 4,
        out_shape=[jax.ShapeDtypeStruct((1, requests * n, D_MODEL), F32)]
        + [jax.ShapeDtypeStruct((requests, n, KV_W), F32)] * 4,
        scratch_shapes=[
            pltpu.VMEM((1, N_HEADS, tm, LANES), BF16),
            pltpu.VMEM((1, tm, MIX_WIDTH), BF16),
            pltpu.VMEM((1, tm, KV_W), BF16),
            pltpu.VMEM((1, KV_W, tm), BF16),
            pltpu.VMEM((1, tm, KV_W), BF16),
            pltpu.VMEM((1, KV_W, tm), BF16),
            pltpu.VMEM((sub, MIX_WIDTH, n), F32),
        ],
        compiler_params=pltpu.CompilerParams(
            dimension_semantics=("arbitrary",), vmem_limit_bytes=VMEM_LIMIT_BYTES),
        name="context_pass",
    )(x_flat, shift, scale, gate, gain, w_in_bf, head_gains, ones_bd, sink, w_out_bf)
    return (outs[0].reshape(requests, n, D_MODEL),) + tuple(outs[1:])


def _rope_tables(n_tokens):
    f32 = np.float32
    rows = n_tokens // GRID_W
    row = np.repeat(np.arange(rows, dtype=f32), GRID_W)
    col = np.tile(np.arange(GRID_W, dtype=f32), rows)
    n_freq = HEAD_DIM // 4
    inv = f32(ROPE_THETA) ** (-np.arange(n_freq, dtype=f32) / f32(n_freq))
    ar = row[:, None] * inv[None, :]
    ac = col[:, None] * inv[None, :]
    ang = np.concatenate([ar, ar, ac, ac], axis=-1).astype(f32)
    sign = np.tile(np.repeat(f32([-1.0, 1.0]), 16), HEAD_DIM // 32)
    cos = np.tile(np.cos(ang), (1, LANES // HEAD_DIM))
    sin_signed = np.tile(np.sin(ang) * sign[None, :], (1, LANES // HEAD_DIM))
    return jnp.asarray(cos, F32), jnp.asarray(sin_signed, F32)


def _ones_blockdiag():
    idx = np.arange(MXU_DIM) // HEAD_DIM
    return jnp.asarray(idx[:, None] == idx[None, :], BF16)


def kernel(x_prompt, x_sample, cache_k_a, cache_v_a, cache_k_b, cache_v_b, c, c_ctx,
           w_mod, b_mod, norm_gain, w_in, qn_a, kn_a, qn_b, kn_b, sink_b, w_out):
    depth = w_in.shape[0]
    batch, seq, _ = x_prompt.shape
    dec_batch, dec_seq, _ = x_sample.shape
    past = cache_k_a.shape[2]

    rope_tables = _rope_tables(dec_seq)
    ones_bd = _ones_blockdiag()
    n_cond = 1 + dec_batch
    cond_rows = -(-n_cond // 8) * 8
    cond = jnp.concatenate(
        [c_ctx[None, :], c, jnp.zeros((cond_rows - n_cond, D_MODEL), F32)], axis=0)

    xp, xs = x_prompt, x_sample
    new_kv = [[], [], [], []]
    tile2 = lambda v: jnp.tile(v, LANES // HEAD_DIM)
    for l in range(depth):
        w_in_bf = w_in[l].astype(BF16)
        w_out_bf = w_out[l].astype(BF16)
        q_scale = HEAD_DIM ** -0.5 * LOG2E
        head_gains = jnp.stack([tile2(qn_a[l]) * q_scale, tile2(kn_a[l]),
                                tile2(qn_b[l]) * q_scale, tile2(kn_b[l])])
        gain = norm_gain[l].reshape(1, D_MODEL)
        sink = sink_b[l].astype(F32)

        m = _modulation(cond, w_mod[l], b_mod[l])
        shift, scale, gate = (m[:, i * D_MODEL:(i + 1) * D_MODEL] for i in range(3))

        sel = lambda v: v[0:1].reshape(1, 1, D_MODEL)
        xp, ka32, va32, kb32, vb32 = _context_pass(
            xp, sel(shift), sel(scale), sel(gate), gain, w_in_bf, head_gains, ones_bd, sink,
            w_out_bf, CTX_TILES_PER_STEP, 3)
        for acc, v in zip(new_kv, (ka32, va32, kb32, vb32)):
            acc.append(v.reshape(batch, seq, N_KV_A, HEAD_DIM))

        sel = lambda v: v[1:n_cond].reshape(dec_batch, 1, D_MODEL)
        q, g, ka, vta, kb, vtb = _project(
            xs, sel(shift), sel(scale), gain, w_in_bf, head_gains, ones_bd, rope_tables, None)
        ctx = tuple(cache[:, l].reshape(dec_batch, past, KV_W)
                    for cache in (cache_k_a, cache_v_a, cache_k_b, cache_v_b))
        xs = _attend(q, g, xs, sel(gate), ka, vta, kb, vtb, ctx, sink, w_out_bf, True, 2,
                     LATENT_TILES_PER_STEP, 2)

    return (xp, xs) + tuple(jnp.stack(v, axis=1) for v in new_kv)
```

```python
import functools

import numpy as np
import jax
import jax.numpy as jnp
from jax import lax
from jax.experimental import pallas as pl
from jax.experimental.pallas import tpu as pltpu

F32 = jnp.float32
BF16 = jnp.bfloat16

D_MODEL = 1024
HEAD_DIM = 64
N_HEADS_A = 8
N_KV_A = 2
N_HEADS_B = 8
N_HEADS = N_HEADS_A + N_HEADS_B
GROUP = N_HEADS_A // N_KV_A
WIDTH_A = N_HEADS_A * HEAD_DIM
WIDTH_B = N_HEADS_B * HEAD_DIM
MIX_WIDTH = WIDTH_A + WIDTH_B
KV_W = N_KV_A * HEAD_DIM
IN_WIDTH = 2 * (2 * WIDTH_A + 2 * KV_W)
GRID_W = 64
WINDOW = 128
ROPE_THETA = 10000.0
EPS = 1e-6
NEG_INF = -1e30
LOG2E = 1.4426950408889634

ROPE_QUARTER = HEAD_DIM // 4

LANES = 128
SUBLANES = 8
MXU_DIM = 256
BF16_ROWS = 16
VMEM_LIMIT_BYTES = 56 * 1024 * 1024

_OFF_QA = 0
_OFF_KA = _OFF_QA + WIDTH_A
_OFF_VA = _OFF_KA + KV_W
_OFF_GA = _OFF_VA + KV_W
_OFF_QB = _OFF_GA + WIDTH_A
_OFF_KB = _OFF_QB + WIDTH_B
_OFF_VB = _OFF_KB + KV_W
_OFF_GB = _OFF_VB + KV_W

TOKEN_TILE = 256
PROJ_TILE = 1024
MOD_COLS = 768
KEY_CHUNK = 256
CTX_TILES_PER_STEP = 4
CTX_EXP_LEAD = 3
LATENT_TILES_PER_STEP = 2
LATENT_HEADS_PER_TASK = 2
LATENT_EXP_LEAD = 2
FILL_AHEAD = 3


def _dot(a, b):
    return jnp.dot(a, b, preferred_element_type=F32)


def _dot_t(a, b):
    return lax.dot_general(a, b, (((1,), (1,)), ((), ())), preferred_element_type=F32)


def _order_after(x, token):
    zero = (pltpu.bitcast(token, jnp.uint32) >> 16) >> 16
    bits = pltpu.bitcast(x, jnp.uint32)
    zero = jnp.concatenate([zero[0:1, :]] * (bits.shape[1] // LANES), axis=1)
    return pltpu.bitcast(bits | jnp.broadcast_to(zero, bits.shape), x.dtype)


def _mod_kernel(cond_ref, w_ref, b_ref, out_ref):
    c = cond_ref[...]
    s = c * jax.nn.sigmoid(c)
    out_ref[...] = _dot(s.astype(BF16), w_ref[...].astype(BF16)) + b_ref[...]


def _modulation(cond, w_mod, b_mod):
    rows = cond.shape[0]
    n_out = w_mod.shape[1]
    bn = MOD_COLS
    return pl.pallas_call(
        _mod_kernel,
        grid=(n_out // bn,),
        in_specs=[
            pl.BlockSpec((rows, D_MODEL), lambda j: (0, 0)),
            pl.BlockSpec((D_MODEL, bn), lambda j: (0, j)),
            pl.BlockSpec((1, bn), lambda j: (0, j)),
        ],
        out_specs=pl.BlockSpec((rows, bn), lambda j: (0, j)),
        out_shape=jax.ShapeDtypeStruct((rows, n_out), F32),
        compiler_params=pltpu.CompilerParams(
            dimension_semantics=("arbitrary",), vmem_limit_bytes=VMEM_LIMIT_BYTES),
        name="modulation",
    )(cond, w_mod, b_mod.reshape(1, n_out))


def _head_rms(blk, ones_blockdiag):
    ss = _dot((blk * blk).astype(BF16), ones_blockdiag)
    return blk * lax.rsqrt(ss * (1.0 / HEAD_DIM) + EPS)


def _rope(blk, cos, sin_signed, low_half):
    up = pltpu.roll(blk, LANES - ROPE_QUARTER, 1)
    down = pltpu.roll(blk, ROPE_QUARTER, 1)
    return blk * cos + jnp.where(low_half, up, down) * sin_signed


def _proj_kernel(*refs, rope, emit_f32):
    it = iter(refs)
    x_ref, shift_ref, scale_ref, gain_ref, w_ref, hg_ref, ones_ref = (next(it) for _ in range(7))
    if rope:
        cos_ref, sin_ref = next(it), next(it)
    q_ref, g_ref, ka_ref, vta_ref, kb_ref, vtb_ref = (next(it) for _ in range(6))
    if emit_f32:
        ka32_ref, va32_ref, kb32_ref, vb32_ref = (next(it) for _ in range(4))

    x = x_ref[0]
    ms = jnp.mean(x * x, axis=-1, keepdims=True)
    h = x * lax.rsqrt(ms + EPS) * gain_ref[...]
    h = h * (1.0 + scale_ref[0]) + shift_ref[0]
    hb = h.astype(BF16)

    lane = lax.broadcasted_iota(jnp.int32, (1, LANES), 1)
    if rope:
        cos = cos_ref[...]
        sin_signed = sin_ref[...]
        low_half = (lane % (2 * ROPE_QUARTER)) < ROPE_QUARTER

    ones256 = ones_ref[...]

    def seg(off, width):
        return _dot(hb, w_ref[:, off:off + width])

    def normed_chunks(p, gains):
        out = []
        for c0 in range(0, p.shape[1], MXU_DIM):
            y = _head_rms(p[:, c0:c0 + MXU_DIM], ones256)
            for c1 in range(0, MXU_DIM, LANES):
                yc = y[:, c1:c1 + LANES] * gains[(c0 + c1) // LANES]
                if rope:
                    yc = _rope(yc, cos, sin_signed, low_half)
                out.append(yc)
        return out

    def store_q(chunks, head0):
        for hh in range(2 * len(chunks)):
            kv = hh // GROUP
            c = chunks[hh // 2]
            if hh % 2 != kv:
                c = pltpu.roll(c, HEAD_DIM, 1)
            keep = (lane < HEAD_DIM) if kv == 0 else (lane >= HEAD_DIM)
            q_ref[0, head0 + hh] = jnp.where(keep, c, 0.0).astype(BF16)

    def silu(v):
        return v * jax.nn.sigmoid(v)

    p_qa = seg(_OFF_QA, WIDTH_A)
    p_qb = seg(_OFF_QB, WIDTH_B)
    qa = normed_chunks(p_qa, [hg_ref[0:1, :]] * (WIDTH_A // LANES))
    p_kva = seg(_OFF_KA, 2 * KV_W)
    store_q(qa, 0)
    qb = normed_chunks(p_qb, [hg_ref[2:3, :]] * (WIDTH_B // LANES))
    p_kvb = seg(_OFF_KB, 2 * KV_W)
    store_q(qb, N_HEADS_A)
    k_both = jnp.concatenate([p_kva[:, 0:KV_W], p_kvb[:, 0:KV_W]], axis=1)
    ka, kb = normed_chunks(k_both, [hg_ref[1:2, :], hg_ref[3:4, :]])
    p_ga = seg(_OFF_GA, WIDTH_A)
    p_gb = seg(_OFF_GB, WIDTH_B)
    ka_ref[0] = ka.astype(BF16)
    kb_ref[0] = kb.astype(BF16)
    va = p_kva[:, KV_W:2 * KV_W]
    vb = p_kvb[:, KV_W:2 * KV_W]
    vta_ref[0] = va.T.astype(BF16)
    vtb_ref[0] = vb.T.astype(BF16)
    g_ref[0, :, 0:WIDTH_A] = silu(p_ga).astype(BF16)
    g_ref[0, :, WIDTH_A:MIX_WIDTH] = silu(p_gb).astype(BF16)
    if emit_f32:
        for ref, val in ((ka32_ref, ka), (va32_ref, va), (kb32_ref, kb), (vb32_ref, vb)):
            ref[...] = val.reshape(ref.shape)


def _project_latent(x, shift, scale, gain, w_in_bf, head_gains, ones_bd, rope_tables):
    b, n, _ = x.shape
    tm = PROJ_TILE
    request = lambda i, j: (i, 0, 0)
    const = lambda i, j: (0, 0)
    k_spec = pl.BlockSpec((1, tm, KV_W), lambda i, j: (i, j, 0))
    vt_spec = pl.BlockSpec((1, KV_W, tm), lambda i, j: (i, 0, j))
    k_shape = jax.ShapeDtypeStruct((b, n, KV_W), BF16)
    vt_shape = jax.ShapeDtypeStruct((b, KV_W, n), BF16)
    return pl.pallas_call(
        functools.partial(_proj_kernel, rope=True, emit_f32=False),
        grid=(b, n // tm),
        in_specs=[
            pl.BlockSpec((1, tm, D_MODEL), lambda i, j: (i, j, 0)),
            pl.BlockSpec((1, 1, D_MODEL), request),
            pl.BlockSpec((1, 1, D_MODEL), request),
            pl.BlockSpec((1, D_MODEL), const),
            pl.BlockSpec((D_MODEL, IN_WIDTH), const),
            pl.BlockSpec((4, LANES), const),
            pl.BlockSpec((MXU_DIM, MXU_DIM), const),
            pl.BlockSpec((tm, LANES), lambda i, j: (j, 0)),
            pl.BlockSpec((tm, LANES), lambda i, j: (j, 0)),
        ],
        out_specs=[
            pl.BlockSpec((1, N_HEADS, tm, LANES), lambda i, j: (i, 0, j, 0)),
            pl.BlockSpec((1, tm, MIX_WIDTH), lambda i, j: (i, j, 0)),
            k_spec, vt_spec, k_spec, vt_spec,
        ],
        out_shape=[
            jax.ShapeDtypeStruct((b, N_HEADS, n, LANES), BF16),
            jax.ShapeDtypeStruct((b, n, MIX_WIDTH), BF16),
            k_shape, vt_shape, k_shape, vt_shape,
        ],
        compiler_params=pltpu.CompilerParams(
            dimension_semantics=("arbitrary", "arbitrary"), vmem_limit_bytes=VMEM_LIMIT_BYTES),
        name="project_latent",
    )(x, shift, scale, gain, w_in_bf, head_gains, ones_bd, *rope_tables)


def _attn_kernel(*refs, n_ctx, windowed, n_lat, tq, gb, sub, own_keys, exp_lead):
    it = iter(refs)
    q_ref, g_ref, x_ref, gate_ref, ka_ref, vta_ref, kb_ref, vtb_ref = (next(it) for _ in range(8))
    if n_ctx:
        cka_ref, cva_ref, ckb_ref, cvb_ref = (next(it) for _ in range(4))
    sink_ref, wout_ref, out_ref = (next(it) for _ in range(3))
    o_scr = next(it)

    width = gb * tq

    if n_ctx:
        ctx_k = {"a": cka_ref[0].astype(BF16), "b": ckb_ref[0].astype(BF16)}
        ctx_vt = {"a": cva_ref[0].T.astype(BF16), "b": cvb_ref[0].T.astype(BF16)}

    band = tq + 2 * WINDOW

    def band_of(u):
        t = pl.program_id(1) * sub + u
        start = jnp.clip(t * tq - WINDOW, 0, n_lat - band)
        start = pl.multiple_of(start, WINDOW)
        kpos = start + lax.broadcasted_iota(jnp.int32, (band, tq), 0)
        qpos = t * tq + lax.broadcasted_iota(jnp.int32, (band, tq), 1)
        bias = jnp.where(jnp.abs(kpos - qpos) <= WINDOW, 0.0, NEG_INF)
        return start, jnp.concatenate([bias] * gb, axis=1)

    ck = KEY_CHUNK

    def spans(total):
        return [(r, min(ck, total - r)) for r in range(0, total, ck)]

    def chunks(u, mixer, kv):
        rows = slice(kv * HEAD_DIM, (kv + 1) * HEAD_DIM)
        k0 = u * n_lat if own_keys else 0
        out = []
        for r, n in spans(n_ctx):
            out.append((n,
                        functools.partial(lambda r, n: ctx_k[mixer][r:r + n, :], r, n),
                        functools.partial(lambda r, n: ctx_vt[mixer][rows, r:r + n], r, n), None))
        if mixer == "a":
            for r, n in spans(n_lat):
                r += k0
                out.append((n,
                            functools.partial(lambda r, n: ka_ref[0, r:r + n, :], r, n),
                            functools.partial(lambda r, n: vta_ref[0, rows, r:r + n], r, n), None))
        elif windowed:
            start, bias = band_of(u)
            for r, n in spans(band):
                out.append((n,
                            functools.partial(
                                lambda r, n: kb_ref[0, pl.ds(k0 + start + r, n), :], r, n),
                            functools.partial(
                                lambda r, n: vtb_ref[0, rows, pl.ds(k0 + start + r, n)], r, n),
                            bias[r:r + n, :]))
        else:
            for r, n in spans(n_lat):
                r += k0
                out.append((n,
                            functools.partial(lambda r, n: kb_ref[0, r:r + n, :], r, n),
                            functools.partial(lambda r, n: vtb_ref[0, rows, r:r + n], r, n), None))
        return out

    def sink_row(h0):
        return jnp.concatenate(
            [jnp.full((1, tq), sink_ref[h0 - N_HEADS_A + j] * LOG2E, F32) for j in range(gb)],
            axis=1)

    tasks = ([("a", h0) for h0 in range(0, N_HEADS_A, gb)]
             + [("b", N_HEADS_A + h0) for h0 in range(0, N_HEADS_B, gb)])
    items = []
    for u in range(sub):
        for ti, (mixer, h0) in enumerate(tasks):
            todo = chunks(u, mixer, (h0 % N_HEADS_A) // GROUP)
            for idx, chunk in enumerate(todo):
                last = idx == len(todo) - 1
                items.append((u, mixer, h0, chunk, idx == 0, last, last and ti == len(tasks) - 1))

    scores = {}
    running = {}
    tokens = []

    def emit_scores(k):
        u, mixer, h0, (_, load_k, _, kbias), _, _, _ = items[k]
        qg = q_ref[0, h0:h0 + gb, u * tq:(u + 1) * tq, :].reshape(width, LANES)
        s = _dot_t(load_k(), qg)
        scores[k] = s if kbias is None else s + kbias

    def emit_merge(u):
        toks = slice(u * tq, (u + 1) * tq)
        o = o_scr[u].T
        gated = (o * g_ref[0, toks, :].astype(F32)).astype(BF16)
        y = _dot(gated, wout_ref[...])
        out_ref[0, toks, :] = x_ref[0, toks, :] + gate_ref[0] * y

    def emit_softmax(k):
        u, mixer, h0, (nk, _, load_vt, _), first, last, tile_done = items[k]
        s = scores.pop(k)
        m_new = jnp.max(s, axis=0, keepdims=True)
        if first:
            if mixer == "b":
                m_new = jnp.maximum(m_new, sink_row(h0))
        else:
            m_old, o_old = running.pop((u, h0))
            m_new = jnp.maximum(m_old, m_new)
        if k >= exp_lead:
            m_new = _order_after(m_new, tokens[k - exp_lead])
        p = jnp.exp2(s - m_new)
        vt_ones = jnp.concatenate([load_vt(), jnp.ones((BF16_ROWS, nk), BF16)], axis=0)
        o = _dot(vt_ones, p.astype(BF16))
        if not first:
            o = o_old * jnp.exp2(m_old - m_new) + o
        tokens.append(o[0:SUBLANES, 0:LANES])
        if not last:
            running[(u, h0)] = (m_new, o)
            return
        l = o[HEAD_DIM:HEAD_DIM + 1, :]
        if mixer == "b":
            l = l + jnp.exp2(sink_row(h0) - m_new)
        o = o[0:HEAD_DIM, :] * (1.0 / l)
        for j in range(gb):
            o_scr[u, (h0 + j) * HEAD_DIM:(h0 + j + 1) * HEAD_DIM, :] = o[:, j * tq:(j + 1) * tq]
        if tile_done:
            emit_merge(u)

    for k in range(len(items) + FILL_AHEAD):
        if k < len(items):
            emit_scores(k)
        if k >= FILL_AHEAD:
            emit_softmax(k - FILL_AHEAD)


def _attend_latent(q, g, x, gate, ka, vta, kb, vtb, ctx, sink, w_out_bf, gb, sub, exp_lead):
    b, n, _ = x.shape
    tq = TOKEN_TILE
    n_ctx = ctx[0].shape[1]
    tok_map = lambda i, j: (i, j, 0)
    request = lambda i, j: (i, 0, 0)
    k_spec = pl.BlockSpec((1, n, KV_W), request)
    vt_spec = pl.BlockSpec((1, KV_W, n), request)
    ctx_spec = pl.BlockSpec((1, n_ctx, KV_W), request)
    return pl.pallas_call(
        functools.partial(_attn_kernel, n_ctx=n_ctx, windowed=True, n_lat=n, tq=tq, gb=gb,
                          sub=sub, own_keys=False, exp_lead=exp_lead),
        grid=(b, n // (sub * tq)),
        in_specs=[
            pl.BlockSpec((1, N_HEADS, sub * tq, LANES), lambda i, j: (i, 0, j, 0)),
            pl.BlockSpec((1, sub * tq, MIX_WIDTH), tok_map),
            pl.BlockSpec((1, sub * tq, D_MODEL), tok_map),
            pl.BlockSpec((1, 1, D_MODEL), request),
            k_spec, vt_spec, k_spec, vt_spec,
            ctx_spec, ctx_spec, ctx_spec, ctx_spec,
            pl.BlockSpec(memory_space=pltpu.SMEM),
            pl.BlockSpec((MIX_WIDTH, D_MODEL), lambda i, j: (0, 0)),
        ],
        out_specs=pl.BlockSpec((1, sub * tq, D_MODEL), tok_map),
        out_shape=jax.ShapeDtypeStruct((b, n, D_MODEL), F32),
        scratch_shapes=[
            pltpu.VMEM((sub, MIX_WIDTH, tq), F32),
        ],
        compiler_params=pltpu.CompilerParams(
            dimension_semantics=("arbitrary", "arbitrary"), vmem_limit_bytes=VMEM_LIMIT_BYTES),
        name="attend_latent",
    )(q, g, x, gate, ka, vta, kb, vtb, *ctx, sink, w_out_bf)


def _ctx_kernel(*refs, tq, sub, exp_lead):
    (x_ref, shift_ref, scale_ref, gate_ref, gain_ref, w_ref, hg_ref, ones_ref, sink_ref, wout_ref,
     out_ref, ka32_ref, va32_ref, kb32_ref, vb32_ref,
     q_scr, g_scr, ka_scr, vta_scr, kb_scr, vtb_scr, o_scr) = refs
    _proj_kernel(x_ref, shift_ref, scale_ref, gain_ref, w_ref, hg_ref, ones_ref,
                 q_scr, g_scr, ka_scr, vta_scr, kb_scr, vtb_scr,
                 ka32_ref, va32_ref, kb32_ref, vb32_ref, rope=False, emit_f32=True)
    _attn_kernel(q_scr, g_scr, x_ref, gate_ref, ka_scr, vta_scr, kb_scr, vtb_scr,
                 sink_ref, wout_ref, out_ref, o_scr,
                 n_ctx=0, windowed=False, n_lat=tq, tq=tq, gb=GROUP, sub=sub, own_keys=True,
                 exp_lead=exp_lead)


def _context_pass(x, shift, scale, gate, gain, w_in_bf, head_gains, ones_bd, sink, w_out_bf,
                  sub, exp_lead):
    requests, n, _ = x.shape
    tm = sub * n
    x_flat = x.reshape(1, requests * n, D_MODEL)
    const = lambda *idx: (lambda i: idx)
    tok_spec = pl.BlockSpec((1, tm, D_MODEL), lambda i: (0, i, 0))
    kv32_spec = pl.BlockSpec((sub, n, KV_W), lambda i: (i, 0, 0))
    outs = pl.pallas_call(
        functools.partial(_ctx_kernel, tq=n, sub=sub, exp_lead=exp_lead),
        grid=(requests // sub,),
        in_specs=[
            tok_spec,
            pl.BlockSpec((1, 1, D_MODEL), const(0, 0, 0)),
            pl.BlockSpec((1, 1, D_MODEL), const(0, 0, 0)),
            pl.BlockSpec((1, 1, D_MODEL), const(0, 0, 0)),
            pl.BlockSpec((1, D_MODEL), const(0, 0)),
            pl.BlockSpec((D_MODEL, IN_WIDTH), const(0, 0)),
            pl.BlockSpec((4, LANES), const(0, 0)),
            pl.BlockSpec((MXU_DIM, MXU_DIM), const(0, 0)),
            pl.BlockSpec(memory_space=pltpu.SMEM),
            pl.BlockSpec((MIX_WIDTH, D_MODEL), const(0, 0)),
        ],
        out_specs=[tok_spec] + [kv32_spec] * 4,
        out_shape=[jax.ShapeDtypeStruct((1, requests * n, D_MODEL), F32)]
        + [jax.ShapeDtypeStruct((requests, n, KV_W), F32)] * 4,
        scratch_shapes=[
            pltpu.VMEM((1, N_HEADS, tm, LANES), BF16),
            pltpu.VMEM((1, tm, MIX_WIDTH), BF16),
            pltpu.VMEM((1, tm, KV_W), BF16),
            pltpu.VMEM((1, KV_W, tm), BF16),
            pltpu.VMEM((1, tm, KV_W), BF16),
            pltpu.VMEM((1, KV_W, tm), BF16),
            pltpu.VMEM((sub, MIX_WIDTH, n), F32),
        ],
        compiler_params=pltpu.CompilerParams(
            dimension_semantics=("arbitrary",), vmem_limit_bytes=VMEM_LIMIT_BYTES),
        name="context_pass",
    )(x_flat, shift, scale, gate, gain, w_in_bf, head_gains, ones_bd, sink, w_out_bf)
    return (outs[0].reshape(requests, n, D_MODEL),) + tuple(outs[1:])


def _rope_tables(n_tokens):
    f32 = np.float32
    rows = n_tokens // GRID_W
    row = np.repeat(np.arange(rows, dtype=f32), GRID_W)
    col = np.tile(np.arange(GRID_W, dtype=f32), rows)
    n_freq = HEAD_DIM // 4
    inv = f32(ROPE_THETA) ** (-np.arange(n_freq, dtype=f32) / f32(n_freq))
    ar = row[:, None] * inv[None, :]
    ac = col[:, None] * inv[None, :]
    ang = np.concatenate([ar, ar, ac, ac], axis=-1).astype(f32)
    sign = np.tile(np.repeat(f32([-1.0, 1.0]), ROPE_QUARTER), 2)
    cos = np.tile(np.cos(ang), (1, LANES // HEAD_DIM))
    sin_signed = np.tile(np.sin(ang) * sign[None, :], (1, LANES // HEAD_DIM))
    return jnp.asarray(cos, F32), jnp.asarray(sin_signed, F32)


def _ones_blockdiag():
    idx = np.arange(MXU_DIM) // HEAD_DIM
    return jnp.asarray(idx[:, None] == idx[None, :], BF16)


def kernel(x_prompt, x_sample, cache_k_a, cache_v_a, cache_k_b, cache_v_b, c, c_ctx,
           w_mod, b_mod, norm_gain, w_in, qn_a, kn_a, qn_b, kn_b, sink_b, w_out):
    depth = w_in.shape[0]
    batch, seq, _ = x_prompt.shape
    dec_batch, dec_seq, _ = x_sample.shape
    past = cache_k_a.shape[2]

    rope_tables = _rope_tables(dec_seq)
    ones_bd = _ones_blockdiag()
    n_cond = 1 + dec_batch
    cond_rows = -(-n_cond // 8) * 8
    cond = jnp.concatenate(
        [c_ctx[None, :], c, jnp.zeros((cond_rows - n_cond, D_MODEL), F32)], axis=0)

    xp, xs = x_prompt, x_sample
    new_kv = [[], [], [], []]
    tile2 = lambda v: jnp.tile(v, LANES // HEAD_DIM)
    for l in range(depth):
        w_in_bf = w_in[l].astype(BF16)
        w_out_bf = w_out[l].astype(BF16)
        q_scale = HEAD_DIM ** -0.5 * LOG2E
        head_gains = jnp.stack([tile2(qn_a[l]) * q_scale, tile2(kn_a[l]),
                                tile2(qn_b[l]) * q_scale, tile2(kn_b[l])])
        gain = norm_gain[l].reshape(1, D_MODEL)
        sink = sink_b[l]

        m = _modulation(cond, w_mod[l], b_mod[l])
        shift, scale, gate = (m[:, i * D_MODEL:(i + 1) * D_MODEL] for i in range(3))

        sel = lambda v: v[0:1].reshape(1, 1, D_MODEL)
        xp, ka32, va32, kb32, vb32 = _context_pass(
            xp, sel(shift), sel(scale), sel(gate), gain, w_in_bf, head_gains, ones_bd, sink,
            w_out_bf, CTX_TILES_PER_STEP, CTX_EXP_LEAD)
        for acc, v in zip(new_kv, (ka32, va32, kb32, vb32)):
            acc.append(v.reshape(batch, seq, N_KV_A, HEAD_DIM))

        sel = lambda v: v[1:n_cond].reshape(dec_batch, 1, D_MODEL)
        q, g, ka, vta, kb, vtb = _project_latent(
            xs, sel(shift), sel(scale), gain, w_in_bf, head_gains, ones_bd, rope_tables)
        ctx = tuple(cache[:, l].reshape(dec_batch, past, KV_W)
                    for cache in (cache_k_a, cache_v_a, cache_k_b, cache_v_b))
        xs = _attend_latent(q, g, xs, sel(gate), ka, vta, kb, vtb, ctx, sink, w_out_bf,
                            LATENT_HEADS_PER_TASK, LATENT_TILES_PER_STEP, LATENT_EXP_LEAD)

    return (xp, xs) + tuple(jnp.stack(v, axis=1) for v in new_kv)
```

```python
import functools

import numpy as np
import jax
import jax.numpy as jnp
from jax import lax
from jax.experimental import pallas as pl
from jax.experimental.pallas import tpu as pltpu

F32 = jnp.float32
BF16 = jnp.bfloat16

D_MODEL = 1024
HEAD_DIM = 64
N_HEADS_A = 8
N_KV_A = 2
N_HEADS_B = 8
N_HEADS = N_HEADS_A + N_HEADS_B
GROUP = N_HEADS_A // N_KV_A
WIDTH_A = N_HEADS_A * HEAD_DIM
WIDTH_B = N_HEADS_B * HEAD_DIM
MIX_WIDTH = WIDTH_A + WIDTH_B
KV_W = N_KV_A * HEAD_DIM
IN_WIDTH = 2 * (2 * WIDTH_A + 2 * KV_W)
GRID_W = 64
WINDOW = 128
ROPE_THETA = 10000.0
EPS = 1e-6
NEG_INF = -1e30
LOG2E = 1.4426950408889634

ROPE_QUARTER = HEAD_DIM // 4

LANES = 128
SUBLANES = 8
MXU_DIM = 256
BF16_ROWS = 16
VMEM_LIMIT_BYTES = 56 * 1024 * 1024

_OFF_QA = 0
_OFF_KA = _OFF_QA + WIDTH_A
_OFF_VA = _OFF_KA + KV_W
_OFF_GA = _OFF_VA + KV_W
_OFF_QB = _OFF_GA + WIDTH_A
_OFF_KB = _OFF_QB + WIDTH_B
_OFF_VB = _OFF_KB + KV_W
_OFF_GB = _OFF_VB + KV_W

TOKEN_TILE = 256
PROJ_TILE = 1024
MOD_COLS = 768
KEY_CHUNK = 256
CTX_TILES_PER_STEP = 4
CTX_EXP_LEAD = None
LATENT_TILES_PER_STEP = 2
LATENT_HEADS_PER_TASK = 2
LATENT_EXP_LEAD = 2
LATENT_FILL_AHEAD = 3
CTX_FILL_AHEAD = 3


def _dot(a, b):
    return jnp.dot(a, b, preferred_element_type=F32)


def _dot_t(a, b):
    return lax.dot_general(a, b, (((1,), (1,)), ((), ())), preferred_element_type=F32)


def _order_after(x, token):
    zero = (pltpu.bitcast(token, jnp.uint32) >> 16) >> 16
    bits = pltpu.bitcast(x, jnp.uint32)
    zero = jnp.concatenate([zero[0:1, :]] * (bits.shape[1] // LANES), axis=1)
    return pltpu.bitcast(bits | jnp.broadcast_to(zero, bits.shape), x.dtype)


def _mod_kernel(cond_ref, w_ref, b_ref, out_ref):
    c = cond_ref[...]
    s = c * jax.nn.sigmoid(c)
    out_ref[...] = _dot(s.astype(BF16), w_ref[...].astype(BF16)) + b_ref[...]


def _modulation(cond, w_mod, b_mod):
    rows = cond.shape[0]
    n_out = w_mod.shape[1]
    bn = MOD_COLS
    return pl.pallas_call(
        _mod_kernel,
        grid=(n_out // bn,),
        in_specs=[
            pl.BlockSpec((rows, D_MODEL), lambda j: (0, 0)),
            pl.BlockSpec((D_MODEL, bn), lambda j: (0, j)),
            pl.BlockSpec((1, bn), lambda j: (0, j)),
        ],
        out_specs=pl.BlockSpec((rows, bn), lambda j: (0, j)),
        out_shape=jax.ShapeDtypeStruct((rows, n_out), F32),
        compiler_params=pltpu.CompilerParams(
            dimension_semantics=("arbitrary",), vmem_limit_bytes=VMEM_LIMIT_BYTES),
        name="modulation",
    )(cond, w_mod, b_mod.reshape(1, n_out))


def _head_rms(blk, ones_blockdiag):
    ss = _dot((blk * blk).astype(BF16), ones_blockdiag)
    return blk * lax.rsqrt(ss * (1.0 / HEAD_DIM) + EPS)


def _rope(blk, cos, sin_signed, low_half):
    up = pltpu.roll(blk, LANES - ROPE_QUARTER, 1)
    down = pltpu.roll(blk, ROPE_QUARTER, 1)
    return blk * cos + jnp.where(low_half, up, down) * sin_signed


def _proj_kernel(*refs, rope, emit_f32):
    it = iter(refs)
    x_ref, shift_ref, scale_ref, gain_ref, w_ref, hg_ref, ones_ref = (next(it) for _ in range(7))
    if rope:
        cos_ref, sin_ref = next(it), next(it)
    q_ref, g_ref, ka_ref, vta_ref, kb_ref, vtb_ref = (next(it) for _ in range(6))
    if emit_f32:
        ka32_ref, va32_ref, kb32_ref, vb32_ref = (next(it) for _ in range(4))

    x = x_ref[0]
    ms = jnp.mean(x * x, axis=-1, keepdims=True)
    h = x * lax.rsqrt(ms + EPS) * gain_ref[...]
    h = h * (1.0 + scale_ref[0]) + shift_ref[0]
    hb = h.astype(BF16)

    lane = lax.broadcasted_iota(jnp.int32, (1, LANES), 1)
    if rope:
        cos = cos_ref[...]
        sin_signed = sin_ref[...]
        low_half = (lane % (2 * ROPE_QUARTER)) < ROPE_QUARTER

    ones256 = ones_ref[...]

    def seg(off, width):
        return _dot(hb, w_ref[:, off:off + width])

    def normed_chunks(p, gains):
        out = []
        for c0 in range(0, p.shape[1], MXU_DIM):
            y = _head_rms(p[:, c0:c0 + MXU_DIM], ones256)
            for c1 in range(0, MXU_DIM, LANES):
                yc = y[:, c1:c1 + LANES] * gains[(c0 + c1) // LANES]
                if rope:
                    yc = _rope(yc, cos, sin_signed, low_half)
                out.append(yc)
        return out

    def store_q(chunks, head0):
        for hh in range(2 * len(chunks)):
            kv = hh // GROUP
            c = chunks[hh // 2]
            if hh % 2 != kv:
                c = pltpu.roll(c, HEAD_DIM, 1)
            keep = (lane < HEAD_DIM) if kv == 0 else (lane >= HEAD_DIM)
            q_ref[0, head0 + hh] = jnp.where(keep, c, 0.0).astype(BF16)

    def silu(v):
        return v * jax.nn.sigmoid(v)

    p_qa = seg(_OFF_QA, WIDTH_A)
    p_qb = seg(_OFF_QB, WIDTH_B)
    qa = normed_chunks(p_qa, [hg_ref[0:1, :]] * (WIDTH_A // LANES))
    p_kva = seg(_OFF_KA, 2 * KV_W)
    store_q(qa, 0)
    qb = normed_chunks(p_qb, [hg_ref[2:3, :]] * (WIDTH_B // LANES))
    p_kvb = seg(_OFF_KB, 2 * KV_W)
    store_q(qb, N_HEADS_A)
    k_both = jnp.concatenate([p_kva[:, 0:KV_W], p_kvb[:, 0:KV_W]], axis=1)
    ka, kb = normed_chunks(k_both, [hg_ref[1:2, :], hg_ref[3:4, :]])
    p_ga = seg(_OFF_GA, WIDTH_A)
    p_gb = seg(_OFF_GB, WIDTH_B)
    ka_ref[0] = ka.astype(BF16)
    kb_ref[0] = kb.astype(BF16)
    va = p_kva[:, KV_W:2 * KV_W]
    vb = p_kvb[:, KV_W:2 * KV_W]
    vta_ref[0] = va.T.astype(BF16)
    vtb_ref[0] = vb.T.astype(BF16)
    g_ref[0, :, 0:WIDTH_A] = silu(p_ga).astype(BF16)
    g_ref[0, :, WIDTH_A:MIX_WIDTH] = silu(p_gb).astype(BF16)
    if emit_f32:
        for ref, val in ((ka32_ref, ka), (va32_ref, va), (kb32_ref, kb), (vb32_ref, vb)):
            ref[...] = val.reshape(ref.shape)


def _project_latent(x, shift, scale, gain, w_in_bf, head_gains, ones_bd, rope_tables):
    b, n, _ = x.shape
    tm = PROJ_TILE
    request = lambda i, j: (i, 0, 0)
    const = lambda i, j: (0, 0)
    k_spec = pl.BlockSpec((1, tm, KV_W), lambda i, j: (i, j, 0))
    vt_spec = pl.BlockSpec((1, KV_W, tm), lambda i, j: (i, 0, j))
    k_shape = jax.ShapeDtypeStruct((b, n, KV_W), BF16)
    vt_shape = jax.ShapeDtypeStruct((b, KV_W, n), BF16)
    return pl.pallas_call(
        functools.partial(_proj_kernel, rope=True, emit_f32=False),
        grid=(b, n // tm),
        in_specs=[
            pl.BlockSpec((1, tm, D_MODEL), lambda i, j: (i, j, 0)),
            pl.BlockSpec((1, 1, D_MODEL), request),
            pl.BlockSpec((1, 1, D_MODEL), request),
            pl.BlockSpec((1, D_MODEL), const),
            pl.BlockSpec((D_MODEL, IN_WIDTH), const),
            pl.BlockSpec((4, LANES), const),
            pl.BlockSpec((MXU_DIM, MXU_DIM), const),
            pl.BlockSpec((tm, LANES), lambda i, j: (j, 0)),
            pl.BlockSpec((tm, LANES), lambda i, j: (j, 0)),
        ],
        out_specs=[
            pl.BlockSpec((1, N_HEADS, tm, LANES), lambda i, j: (i, 0, j, 0)),
            pl.BlockSpec((1, tm, MIX_WIDTH), lambda i, j: (i, j, 0)),
            k_spec, vt_spec, k_spec, vt_spec,
        ],
        out_shape=[
            jax.ShapeDtypeStruct((b, N_HEADS, n, LANES), BF16),
            jax.ShapeDtypeStruct((b, n, MIX_WIDTH), BF16),
            k_shape, vt_shape, k_shape, vt_shape,
        ],
        compiler_params=pltpu.CompilerParams(
            dimension_semantics=("arbitrary", "arbitrary"), vmem_limit_bytes=VMEM_LIMIT_BYTES),
        name="project_latent",
    )(x, shift, scale, gain, w_in_bf, head_gains, ones_bd, *rope_tables)


def _attn_kernel(*refs, n_ctx, windowed, n_lat, tq, gb, sub, own_keys, exp_lead, fill_ahead):
    it = iter(refs)
    q_ref, g_ref, x_ref, gate_ref, ka_ref, vta_ref, kb_ref, vtb_ref = (next(it) for _ in range(8))
    if n_ctx:
        cka_ref, cva_ref, ckb_ref, cvb_ref = (next(it) for _ in range(4))
    sink_ref, wout_ref, out_ref = (next(it) for _ in range(3))
    o_scr = next(it)

    width = gb * tq

    if n_ctx:
        ctx_k = {"a": cka_ref[0].astype(BF16), "b": ckb_ref[0].astype(BF16)}
        ctx_vt = {"a": cva_ref[0].T.astype(BF16), "b": cvb_ref[0].T.astype(BF16)}

    band = tq + 2 * WINDOW

    def band_of(u):
        t = pl.program_id(1) * sub + u
        start = jnp.clip(t * tq - WINDOW, 0, n_lat - band)
        start = pl.multiple_of(start, WINDOW)
        kpos = start + lax.broadcasted_iota(jnp.int32, (band, tq), 0)
        qpos = t * tq + lax.broadcasted_iota(jnp.int32, (band, tq), 1)
        bias = jnp.where(jnp.abs(kpos - qpos) <= WINDOW, 0.0, NEG_INF)
        return start, jnp.concatenate([bias] * gb, axis=1)

    ck = KEY_CHUNK

    def spans(total):
        return [(r, min(ck, total - r)) for r in range(0, total, ck)]

    def chunks(u, mixer, kv):
        rows = slice(kv * HEAD_DIM, (kv + 1) * HEAD_DIM)
        k0 = u * n_lat if own_keys else 0
        out = []
        for r, n in spans(n_ctx):
            out.append((n,
                        functools.partial(lambda r, n: ctx_k[mixer][r:r + n, :], r, n),
                        functools.partial(lambda r, n: ctx_vt[mixer][rows, r:r + n], r, n), None))
        if mixer == "a":
            for r, n in spans(n_lat):
                r += k0
                out.append((n,
                            functools.partial(lambda r, n: ka_ref[0, r:r + n, :], r, n),
                            functools.partial(lambda r, n: vta_ref[0, rows, r:r + n], r, n), None))
        elif windowed:
            start, bias = band_of(u)
            for r, n in spans(band):
                out.append((n,
                            functools.partial(
                                lambda r, n: kb_ref[0, pl.ds(k0 + start + r, n), :], r, n),
                            functools.partial(
                                lambda r, n: vtb_ref[0, rows, pl.ds(k0 + start + r, n)], r, n),
                            bias[r:r + n, :]))
        else:
            for r, n in spans(n_lat):
                r += k0
                out.append((n,
                            functools.partial(lambda r, n: kb_ref[0, r:r + n, :], r, n),
                            functools.partial(lambda r, n: vtb_ref[0, rows, r:r + n], r, n), None))
        return out

    def sink_row(h0):
        return jnp.concatenate(
            [jnp.full((1, tq), sink_ref[h0 - N_HEADS_A + j] * LOG2E, F32) for j in range(gb)],
            axis=1)

    tasks = ([("a", h0) for h0 in range(0, N_HEADS_A, gb)]
             + [("b", N_HEADS_A + h0) for h0 in range(0, N_HEADS_B, gb)])
    items = []
    for u in range(sub):
        for ti, (mixer, h0) in enumerate(tasks):
            todo = chunks(u, mixer, (h0 % N_HEADS_A) // GROUP)
            for idx, chunk in enumerate(todo):
                last = idx == len(todo) - 1
                items.append((u, mixer, h0, chunk, idx == 0, last, last and ti == len(tasks) - 1))

    scores = {}
    running = {}
    tokens = []

    def emit_scores(k):
        u, mixer, h0, (_, load_k, _, kbias), _, _, _ = items[k]
        qg = q_ref[0, h0:h0 + gb, u * tq:(u + 1) * tq, :].reshape(width, LANES)
        s = _dot_t(load_k(), qg)
        scores[k] = s if kbias is None else s + kbias

    def emit_merge(u):
        toks = slice(u * tq, (u + 1) * tq)
        o = o_scr[u].T
        gated = (o * g_ref[0, toks, :].astype(F32)).astype(BF16)
        y = _dot(gated, wout_ref[...])
        out_ref[0, toks, :] = x_ref[0, toks, :] + gate_ref[0] * y

    def emit_softmax(k):
        u, mixer, h0, (nk, _, load_vt, _), first, last, tile_done = items[k]
        s = scores.pop(k)
        m_new = jnp.max(s, axis=0, keepdims=True)
        if first:
            if mixer == "b":
                m_new = jnp.maximum(m_new, sink_row(h0))
        else:
            m_old, o_old = running.pop((u, h0))
            m_new = jnp.maximum(m_old, m_new)
        if exp_lead is not None and k >= exp_lead:
            m_new = _order_after(m_new, tokens[k - exp_lead])
        p = jnp.exp2(s - m_new)
        vt_ones = jnp.concatenate([load_vt(), jnp.ones((BF16_ROWS, nk), BF16)], axis=0)
        o = _dot(vt_ones, p.astype(BF16))
        if not first:
            o = o_old * jnp.exp2(m_old - m_new) + o
        tokens.append(o[0:SUBLANES, 0:LANES])
        if not last:
            running[(u, h0)] = (m_new, o)
            return
        l = o[HEAD_DIM:HEAD_DIM + 1, :]
        if mixer == "b":
            l = l + jnp.exp2(sink_row(h0) - m_new)
        o = o[0:HEAD_DIM, :] * (1.0 / l)
        for j in range(gb):
            o_scr[u, (h0 + j) * HEAD_DIM:(h0 + j + 1) * HEAD_DIM, :] = o[:, j * tq:(j + 1) * tq]
        if tile_done:
            emit_merge(u)

    for k in range(len(items) + fill_ahead):
        if k < len(items):
            emit_scores(k)
        if k >= fill_ahead:
            emit_softmax(k - fill_ahead)


def _attend_latent(q, g, x, gate, ka, vta, kb, vtb, ctx, sink, w_out_bf, gb, sub, exp_lead):
    b, n, _ = x.shape
    tq = TOKEN_TILE
    n_ctx = ctx[0].shape[1]
    tok_map = lambda i, j: (i, j, 0)
    request = lambda i, j: (i, 0, 0)
    k_spec = pl.BlockSpec((1, n, KV_W), request)
    vt_spec = pl.BlockSpec((1, KV_W, n), request)
    ctx_spec = pl.BlockSpec((1, n_ctx, KV_W), request)
    return pl.pallas_call(
        functools.partial(_attn_kernel, n_ctx=n_ctx, windowed=True, n_lat=n, tq=tq, gb=gb,
                          sub=sub, own_keys=False, exp_lead=exp_lead,
                          fill_ahead=LATENT_FILL_AHEAD),
        grid=(b, n // (sub * tq)),
        in_specs=[
            pl.BlockSpec((1, N_HEADS, sub * tq, LANES), lambda i, j: (i, 0, j, 0)),
            pl.BlockSpec((1, sub * tq, MIX_WIDTH), tok_map),
            pl.BlockSpec((1, sub * tq, D_MODEL), tok_map),
            pl.BlockSpec((1, 1, D_MODEL), request),
            k_spec, vt_spec, k_spec, vt_spec,
            ctx_spec, ctx_spec, ctx_spec, ctx_spec,
            pl.BlockSpec(memory_space=pltpu.SMEM),
            pl.BlockSpec((MIX_WIDTH, D_MODEL), lambda i, j: (0, 0)),
        ],
        out_specs=pl.BlockSpec((1, sub * tq, D_MODEL), tok_map),
        out_shape=jax.ShapeDtypeStruct((b, n, D_MODEL), F32),
        scratch_shapes=[
            pltpu.VMEM((sub, MIX_WIDTH, tq), F32),
        ],
        compiler_params=pltpu.CompilerParams(
            dimension_semantics=("arbitrary", "arbitrary"), vmem_limit_bytes=VMEM_LIMIT_BYTES),
        name="attend_latent",
    )(q, g, x, gate, ka, vta, kb, vtb, *ctx, sink, w_out_bf)


def _ctx_kernel(*refs, tq, sub, exp_lead):
    (x_ref, shift_ref, scale_ref, gate_ref, gain_ref, w_ref, hg_ref, ones_ref, sink_ref, wout_ref,
     out_ref, ka32_ref, va32_ref, kb32_ref, vb32_ref,
     q_scr, g_scr, ka_scr, vta_scr, kb_scr, vtb_scr, o_scr) = refs
    _proj_kernel(x_ref, shift_ref, scale_ref, gain_ref, w_ref, hg_ref, ones_ref,
                 q_scr, g_scr, ka_scr, vta_scr, kb_scr, vtb_scr,
                 ka32_ref, va32_ref, kb32_ref, vb32_ref, rope=False, emit_f32=True)
    _attn_kernel(q_scr, g_scr, x_ref, gate_ref, ka_scr, vta_scr, kb_scr, vtb_scr,
                 sink_ref, wout_ref, out_ref, o_scr,
                 n_ctx=0, windowed=False, n_lat=tq, tq=tq, gb=GROUP, sub=sub, own_keys=True,
                 exp_lead=exp_lead, fill_ahead=CTX_FILL_AHEAD)


def _context_pass(x, shift, scale, gate, gain, w_in_bf, head_gains, ones_bd, sink, w_out_bf,
                  sub, exp_lead):
    requests, n, _ = x.shape
    tm = sub * n
    x_flat = x.reshape(1, requests * n, D_MODEL)
    const = lambda *idx: (lambda i: idx)
    tok_spec = pl.BlockSpec((1, tm, D_MODEL), lambda i: (0, i, 0))
    kv32_spec = pl.BlockSpec((sub, n, KV_W), lambda i: (i, 0, 0))
    outs = pl.pallas_call(
        functools.partial(_ctx_kernel, tq=n, sub=sub, exp_lead=exp_lead),
        grid=(requests // sub,),
        in_specs=[
            tok_spec,
            pl.BlockSpec((1, 1, D_MODEL), const(0, 0, 0)),
            pl.BlockSpec((1, 1, D_MODEL), const(0, 0, 0)),
            pl.BlockSpec((1, 1, D_MODEL), const(0, 0, 0)),
            pl.BlockSpec((1, D_MODEL), const(0, 0)),
            pl.BlockSpec((D_MODEL, IN_WIDTH), const(0, 0)),
            pl.BlockSpec((4, LANES), const(0, 0)),
            pl.BlockSpec((MXU_DIM, MXU_DIM), const(0, 0)),
            pl.BlockSpec(memory_space=pltpu.SMEM),
            pl.BlockSpec((MIX_WIDTH, D_MODEL), const(0, 0)),
        ],
        out_specs=[tok_spec] + [kv32_spec] * 4,
        out_shape=[jax.ShapeDtypeStruct((1, requests * n, D_MODEL), F32)]
        + [jax.ShapeDtypeStruct((requests, n, KV_W), F32)] * 4,
        scratch_shapes=[
            pltpu.VMEM((1, N_HEADS, tm, LANES), BF16),
            pltpu.VMEM((1, tm, MIX_WIDTH), BF16),
            pltpu.VMEM((1, tm, KV_W), BF16),
            pltpu.VMEM((1, KV_W, tm), BF16),
            pltpu.VMEM((1, tm, KV_W), BF16),
            pltpu.VMEM((1, KV_W, tm), BF16),
            pltpu.VMEM((sub, MIX_WIDTH, n), F32),
        ],
        compiler_params=pltpu.CompilerParams(
            dimension_semantics=("arbitrary",), vmem_limit_bytes=VMEM_LIMIT_BYTES),
        name="context_pass",
    )(x_flat, shift, scale, gate, gain, w_in_bf, head_gains, ones_bd, sink, w_out_bf)
    return (outs[0].reshape(requests, n, D_MODEL),) + tuple(outs[1:])


def _rope_tables(n_tokens):
    f32 = np.float32
    rows = n_tokens // GRID_W
    row = np.repeat(np.arange(rows, dtype=f32), GRID_W)
    col = np.tile(np.arange(GRID_W, dtype=f32), rows)
    n_freq = HEAD_DIM // 4
    inv = f32(ROPE_THETA) ** (-np.arange(n_freq, dtype=f32) / f32(n_freq))
    ar = row[:, None] * inv[None, :]
    ac = col[:, None] * inv[None, :]
    ang = np.concatenate([ar, ar, ac, ac], axis=-1).astype(f32)
    sign = np.tile(np.repeat(f32([-1.0, 1.0]), ROPE_QUARTER), 2)
    cos = np.tile(np.cos(ang), (1, LANES // HEAD_DIM))
    sin_signed = np.tile(np.sin(ang) * sign[None, :], (1, LANES // HEAD_DIM))
    return jnp.asarray(cos, F32), jnp.asarray(sin_signed, F32)


def _ones_blockdiag():
    idx = np.arange(MXU_DIM) // HEAD_DIM
    return jnp.asarray(idx[:, None] == idx[None, :], BF16)


def kernel(x_prompt, x_sample, cache_k_a, cache_v_a, cache_k_b, cache_v_b, c, c_ctx,
           w_mod, b_mod, norm_gain, w_in, qn_a, kn_a, qn_b, kn_b, sink_b, w_out):
    depth = w_in.shape[0]
    batch, seq, _ = x_prompt.shape
    dec_batch, dec_seq, _ = x_sample.shape
    past = cache_k_a.shape[2]

    rope_tables = _rope_tables(dec_seq)
    ones_bd = _ones_blockdiag()
    n_cond = 1 + dec_batch
    cond_rows = -(-n_cond // 8) * 8
    cond = jnp.concatenate(
        [c_ctx[None, :], c, jnp.zeros((cond_rows - n_cond, D_MODEL), F32)], axis=0)

    xp, xs = x_prompt, x_sample
    new_kv = [[], [], [], []]
    tile2 = lambda v: jnp.tile(v, LANES // HEAD_DIM)
    for l in range(depth):
        w_in_bf = w_in[l].astype(BF16)
        w_out_bf = w_out[l].astype(BF16)
        q_scale = HEAD_DIM ** -0.5 * LOG2E
        head_gains = jnp.stack([tile2(qn_a[l]) * q_scale, tile2(kn_a[l]),
                                tile2(qn_b[l]) * q_scale, tile2(kn_b[l])])
        gain = norm_gain[l].reshape(1, D_MODEL)
        sink = sink_b[l]

        m = _modulation(cond, w_mod[l], b_mod[l])
        shift, scale, gate = (m[:, i * D_MODEL:(i + 1) * D_MODEL] for i in range(3))

        sel = lambda v: v[0:1].reshape(1, 1, D_MODEL)
        xp, ka32, va32, kb32, vb32 = _context_pass(
            xp, sel(shift), sel(scale), sel(gate), gain, w_in_bf, head_gains, ones_bd, sink,
            w_out_bf, CTX_TILES_PER_STEP, CTX_EXP_LEAD)
        for acc, v in zip(new_kv, (ka32, va32, kb32, vb32)):
            acc.append(v.reshape(batch, seq, N_KV_A, HEAD_DIM))

        sel = lambda v: v[1:n_cond].reshape(dec_batch, 1, D_MODEL)
        q, g, ka, vta, kb, vtb = _project_latent(
            xs, sel(shift), sel(scale), gain, w_in_bf, head_gains, ones_bd, rope_tables)
        ctx = tuple(cache[:, l].reshape(dec_batch, past, KV_W)
                    for cache in (cache_k_a, cache_v_a, cache_k_b, cache_v_b))
        xs = _attend_latent(q, g, xs, sel(gate), ka, vta, kb, vtb, ctx, sink, w_out_bf,
                            LATENT_HEADS_PER_TASK, LATENT_TILES_PER_STEP, LATENT_EXP_LEAD)

    return (xp, xs) + tuple(jnp.stack(v, axis=1) for v in new_kv)
```

```python
import functools

import numpy as np
import jax
import jax.numpy as jnp
from jax import lax
from jax.experimental import pallas as pl
from jax.experimental.pallas import tpu as pltpu

F32 = jnp.float32
BF16 = jnp.bfloat16

D_MODEL = 1024
HEAD_DIM = 64
N_HEADS_A = 8
N_KV_A = 2
N_HEADS_B = 8
N_HEADS = N_HEADS_A + N_HEADS_B
GROUP = N_HEADS_A // N_KV_A
WIDTH_A = N_HEADS_A * HEAD_DIM
WIDTH_B = N_HEADS_B * HEAD_DIM
MIX_WIDTH = WIDTH_A + WIDTH_B
KV_W = N_KV_A * HEAD_DIM
IN_WIDTH = 2 * (2 * WIDTH_A + 2 * KV_W)
GRID_W = 64
WINDOW = 128
ROPE_THETA = 10000.0
EPS = 1e-6
NEG_INF = -1e30
LOG2E = 1.4426950408889634

ROPE_QUARTER = HEAD_DIM // 4

LANES = 128
SUBLANES = 8
MXU_DIM = 256
BF16_ROWS = 16
VMEM_LIMIT_BYTES = 56 * 1024 * 1024

_OFF_QA = 0
_OFF_KA = _OFF_QA + WIDTH_A
_OFF_VA = _OFF_KA + KV_W
_OFF_GA = _OFF_VA + KV_W
_OFF_QB = _OFF_GA + WIDTH_A
_OFF_KB = _OFF_QB + WIDTH_B
_OFF_VB = _OFF_KB + KV_W
_OFF_GB = _OFF_VB + KV_W

TOKEN_TILE = 256
PROJ_TILE = 1024
MOD_COLS = 768
KEY_CHUNK = 256
CTX_TILES_PER_STEP = 4
CTX_EXP_LEAD = None
LATENT_TILES_PER_STEP = 2
LATENT_HEADS_PER_TASK = 2
LATENT_EXP_LEAD = 2
LATENT_FILL_AHEAD = 2
CTX_FILL_AHEAD = 3


def _dot(a, b):
    return jnp.dot(a, b, preferred_element_type=F32)


def _dot_t(a, b):
    return lax.dot_general(a, b, (((1,), (1,)), ((), ())), preferred_element_type=F32)


def _order_after(x, token):
    zero = (pltpu.bitcast(token, jnp.uint32) >> 16) >> 16
    bits = pltpu.bitcast(x, jnp.uint32)
    zero = jnp.concatenate([zero[0:1, :]] * (bits.shape[1] // LANES), axis=1)
    return pltpu.bitcast(bits | jnp.broadcast_to(zero, bits.shape), x.dtype)


def _mod_kernel(cond_ref, w_ref, b_ref, out_ref):
    c = cond_ref[...]
    s = c * jax.nn.sigmoid(c)
    out_ref[...] = _dot(s.astype(BF16), w_ref[...].astype(BF16)) + b_ref[...]


def _modulation(cond, w_mod, b_mod):
    rows = cond.shape[0]
    n_out = w_mod.shape[1]
    bn = MOD_COLS
    return pl.pallas_call(
        _mod_kernel,
        grid=(n_out // bn,),
        in_specs=[
            pl.BlockSpec((rows, D_MODEL), lambda j: (0, 0)),
            pl.BlockSpec((D_MODEL, bn), lambda j: (0, j)),
            pl.BlockSpec((1, bn), lambda j: (0, j)),
        ],
        out_specs=pl.BlockSpec((rows, bn), lambda j: (0, j)),
        out_shape=jax.ShapeDtypeStruct((rows, n_out), F32),
        compiler_params=pltpu.CompilerParams(
            dimension_semantics=("arbitrary",), vmem_limit_bytes=VMEM_LIMIT_BYTES),
        name="modulation",
    )(cond, w_mod, b_mod.reshape(1, n_out))


def _head_rms(blk, ones_blockdiag):
    ss = _dot((blk * blk).astype(BF16), ones_blockdiag)
    return blk * lax.rsqrt(ss * (1.0 / HEAD_DIM) + EPS)


def _rope(blk, cos, sin_signed, low_half):
    up = pltpu.roll(blk, LANES - ROPE_QUARTER, 1)
    down = pltpu.roll(blk, ROPE_QUARTER, 1)
    return blk * cos + jnp.where(low_half, up, down) * sin_signed


def _proj_kernel(*refs, rope, emit_f32):
    it = iter(refs)
    x_ref, shift_ref, scale_ref, gain_ref, w_ref, hg_ref, ones_ref = (next(it) for _ in range(7))
    if rope:
        cos_ref, sin_ref = next(it), next(it)
    q_ref, g_ref, ka_ref, vta_ref, kb_ref, vtb_ref = (next(it) for _ in range(6))
    if emit_f32:
        ka32_ref, va32_ref, kb32_ref, vb32_ref = (next(it) for _ in range(4))

    x = x_ref[0]
    ms = jnp.mean(x * x, axis=-1, keepdims=True)
    h = x * lax.rsqrt(ms + EPS) * gain_ref[...]
    h = h * (1.0 + scale_ref[0]) + shift_ref[0]
    hb = h.astype(BF16)

    lane = lax.broadcasted_iota(jnp.int32, (1, LANES), 1)
    if rope:
        cos = cos_ref[...]
        sin_signed = sin_ref[...]
        low_half = (lane % (2 * ROPE_QUARTER)) < ROPE_QUARTER

    ones256 = ones_ref[...]

    def seg(off, width):
        return _dot(hb, w_ref[:, off:off + width])

    def normed_chunks(p, gains):
        out = []
        for c0 in range(0, p.shape[1], MXU_DIM):
            y = _head_rms(p[:, c0:c0 + MXU_DIM], ones256)
            for c1 in range(0, MXU_DIM, LANES):
                yc = y[:, c1:c1 + LANES] * gains[(c0 + c1) // LANES]
                if rope:
                    yc = _rope(yc, cos, sin_signed, low_half)
                out.append(yc)
        return out

    def store_q(chunks, head0):
        for hh in range(2 * len(chunks)):
            kv = hh // GROUP
            c = chunks[hh // 2]
            if hh % 2 != kv:
                c = pltpu.roll(c, HEAD_DIM, 1)
            keep = (lane < HEAD_DIM) if kv == 0 else (lane >= HEAD_DIM)
            q_ref[0, head0 + hh] = jnp.where(keep, c, 0.0).astype(BF16)

    def silu(v):
        return v * jax.nn.sigmoid(v)

    p_qa = seg(_OFF_QA, WIDTH_A)
    p_qb = seg(_OFF_QB, WIDTH_B)
    qa = normed_chunks(p_qa, [hg_ref[0:1, :]] * (WIDTH_A // LANES))
    p_kva = seg(_OFF_KA, 2 * KV_W)
    store_q(qa, 0)
    qb = normed_chunks(p_qb, [hg_ref[2:3, :]] * (WIDTH_B // LANES))
    p_kvb = seg(_OFF_KB, 2 * KV_W)
    store_q(qb, N_HEADS_A)
    k_both = jnp.concatenate([p_kva[:, 0:KV_W], p_kvb[:, 0:KV_W]], axis=1)
    ka, kb = normed_chunks(k_both, [hg_ref[1:2, :], hg_ref[3:4, :]])
    p_ga = seg(_OFF_GA, WIDTH_A)
    p_gb = seg(_OFF_GB, WIDTH_B)
    ka_ref[0] = ka.astype(BF16)
    kb_ref[0] = kb.astype(BF16)
    va = p_kva[:, KV_W:2 * KV_W]
    vb = p_kvb[:, KV_W:2 * KV_W]
    vta_ref[0] = va.T.astype(BF16)
    vtb_ref[0] = vb.T.astype(BF16)
    g_ref[0, :, 0:WIDTH_A] = silu(p_ga).astype(BF16)
    g_ref[0, :, WIDTH_A:MIX_WIDTH] = silu(p_gb).astype(BF16)
    if emit_f32:
        for ref, val in ((ka32_ref, ka), (va32_ref, va), (kb32_ref, kb), (vb32_ref, vb)):
            ref[...] = val.reshape(ref.shape)


def _project_latent(x, shift, scale, gain, w_in_bf, head_gains, ones_bd, rope_tables):
    b, n, _ = x.shape
    tm = PROJ_TILE
    request = lambda i, j: (i, 0, 0)
    const = lambda i, j: (0, 0)
    k_spec = pl.BlockSpec((1, tm, KV_W), lambda i, j: (i, j, 0))
    vt_spec = pl.BlockSpec((1, KV_W, tm), lambda i, j: (i, 0, j))
    k_shape = jax.ShapeDtypeStruct((b, n, KV_W), BF16)
    vt_shape = jax.ShapeDtypeStruct((b, KV_W, n), BF16)
    return pl.pallas_call(
        functools.partial(_proj_kernel, rope=True, emit_f32=False),
        grid=(b, n // tm),
        in_specs=[
            pl.BlockSpec((1, tm, D_MODEL), lambda i, j: (i, j, 0)),
            pl.BlockSpec((1, 1, D_MODEL), request),
            pl.BlockSpec((1, 1, D_MODEL), request),
            pl.BlockSpec((1, D_MODEL), const),
            pl.BlockSpec((D_MODEL, IN_WIDTH), const),
            pl.BlockSpec((4, LANES), const),
            pl.BlockSpec((MXU_DIM, MXU_DIM), const),
            pl.BlockSpec((tm, LANES), lambda i, j: (j, 0)),
            pl.BlockSpec((tm, LANES), lambda i, j: (j, 0)),
        ],
        out_specs=[
            pl.BlockSpec((1, N_HEADS, tm, LANES), lambda i, j: (i, 0, j, 0)),
            pl.BlockSpec((1, tm, MIX_WIDTH), lambda i, j: (i, j, 0)),
            k_spec, vt_spec, k_spec, vt_spec,
        ],
        out_shape=[
            jax.ShapeDtypeStruct((b, N_HEADS, n, LANES), BF16),
            jax.ShapeDtypeStruct((b, n, MIX_WIDTH), BF16),
            k_shape, vt_shape, k_shape, vt_shape,
        ],
        compiler_params=pltpu.CompilerParams(
            dimension_semantics=("arbitrary", "arbitrary"), vmem_limit_bytes=VMEM_LIMIT_BYTES),
        name="project_latent",
    )(x, shift, scale, gain, w_in_bf, head_gains, ones_bd, *rope_tables)


def _attn_kernel(*refs, n_ctx, windowed, n_lat, tq, gb, sub, own_keys, exp_lead, fill_ahead):
    it = iter(refs)
    q_ref, g_ref, x_ref, gate_ref, ka_ref, vta_ref, kb_ref, vtb_ref = (next(it) for _ in range(8))
    if n_ctx:
        cka_ref, cva_ref, ckb_ref, cvb_ref = (next(it) for _ in range(4))
    sink_ref, wout_ref, out_ref = (next(it) for _ in range(3))
    o_scr = next(it)

    width = gb * tq

    if n_ctx:
        ctx_k = {"a": cka_ref[0].astype(BF16), "b": ckb_ref[0].astype(BF16)}
        ctx_vt = {"a": cva_ref[0].T.astype(BF16), "b": cvb_ref[0].T.astype(BF16)}

    band = tq + 2 * WINDOW

    def band_of(u):
        t = pl.program_id(1) * sub + u
        start = jnp.clip(t * tq - WINDOW, 0, n_lat - band)
        start = pl.multiple_of(start, WINDOW)
        kpos = start + lax.broadcasted_iota(jnp.int32, (band, tq), 0)
        qpos = t * tq + lax.broadcasted_iota(jnp.int32, (band, tq), 1)
        bias = jnp.where(jnp.abs(kpos - qpos) <= WINDOW, 0.0, NEG_INF)
        return start, jnp.concatenate([bias] * gb, axis=1)

    ck = KEY_CHUNK

    def spans(total):
        return [(r, min(ck, total - r)) for r in range(0, total, ck)]

    def chunks(u, mixer, kv):
        rows = slice(kv * HEAD_DIM, (kv + 1) * HEAD_DIM)
        k0 = u * n_lat if own_keys else 0
        out = []
        for r, n in spans(n_ctx):
            out.append((n,
                        functools.partial(lambda r, n: ctx_k[mixer][r:r + n, :], r, n),
                        functools.partial(lambda r, n: ctx_vt[mixer][rows, r:r + n], r, n), None))
        if mixer == "a":
            for r, n in spans(n_lat):
                r += k0
                out.append((n,
                            functools.partial(lambda r, n: ka_ref[0, r:r + n, :], r, n),
                            functools.partial(lambda r, n: vta_ref[0, rows, r:r + n], r, n), None))
        elif windowed:
            start, bias = band_of(u)
            for r, n in spans(band):
                out.append((n,
                            functools.partial(
                                lambda r, n: kb_ref[0, pl.ds(k0 + start + r, n), :], r, n),
                            functools.partial(
                                lambda r, n: vtb_ref[0, rows, pl.ds(k0 + start + r, n)], r, n),
                            bias[r:r + n, :]))
        else:
            for r, n in spans(n_lat):
                r += k0
                out.append((n,
                            functools.partial(lambda r, n: kb_ref[0, r:r + n, :], r, n),
                            functools.partial(lambda r, n: vtb_ref[0, rows, r:r + n], r, n), None))
        return out

    def sink_row(h0):
        return jnp.concatenate(
            [jnp.full((1, tq), sink_ref[h0 - N_HEADS_A + j] * LOG2E, F32) for j in range(gb)],
            axis=1)

    tasks = ([("a", h0) for h0 in range(0, N_HEADS_A, gb)]
             + [("b", N_HEADS_A + h0) for h0 in range(0, N_HEADS_B, gb)])
    items = []
    for u in range(sub):
        for ti, (mixer, h0) in enumerate(tasks):
            todo = chunks(u, mixer, (h0 % N_HEADS_A) // GROUP)
            for idx, chunk in enumerate(todo):
                last = idx == len(todo) - 1
                items.append((u, mixer, h0, chunk, idx == 0, last, last and ti == len(tasks) - 1))

    scores = {}
    running = {}
    tokens = []

    def emit_scores(k):
        u, mixer, h0, (_, load_k, _, kbias), _, _, _ = items[k]
        qg = q_ref[0, h0:h0 + gb, u * tq:(u + 1) * tq, :].reshape(width, LANES)
        s = _dot_t(load_k(), qg)
        scores[k] = s if kbias is None else s + kbias

    def emit_merge(u):
        toks = slice(u * tq, (u + 1) * tq)
        o = o_scr[u].T
        gated = (o * g_ref[0, toks, :].astype(F32)).astype(BF16)
        y = _dot(gated, wout_ref[...])
        out_ref[0, toks, :] = x_ref[0, toks, :] + gate_ref[0] * y

    def emit_softmax(k):
        u, mixer, h0, (nk, _, load_vt, _), first, last, tile_done = items[k]
        s = scores.pop(k)
        m_new = jnp.max(s, axis=0, keepdims=True)
        if first:
            if mixer == "b":
                m_new = jnp.maximum(m_new, sink_row(h0))
        else:
            m_old, o_old = running.pop((u, h0))
            m_new = jnp.maximum(m_old, m_new)
        if exp_lead is not None and k >= exp_lead:
            m_new = _order_after(m_new, tokens[k - exp_lead])
        p = jnp.exp2(s - m_new)
        vt_ones = jnp.concatenate([load_vt(), jnp.ones((BF16_ROWS, nk), BF16)], axis=0)
        o = _dot(vt_ones, p.astype(BF16))
        if not first:
            o = o_old * jnp.exp2(m_old - m_new) + o
        tokens.append(o[0:SUBLANES, 0:LANES])
        if not last:
            running[(u, h0)] = (m_new, o)
            return
        l = o[HEAD_DIM:HEAD_DIM + 1, :]
        if mixer == "b":
            l = l + jnp.exp2(sink_row(h0) - m_new)
        o = o[0:HEAD_DIM, :] * (1.0 / l)
        for j in range(gb):
            o_scr[u, (h0 + j) * HEAD_DIM:(h0 + j + 1) * HEAD_DIM, :] = o[:, j * tq:(j + 1) * tq]
        if tile_done:
            emit_merge(u)

    for k in range(len(items) + fill_ahead):
        if k < len(items):
            emit_scores(k)
        if k >= fill_ahead:
            emit_softmax(k - fill_ahead)


def _attend_latent(q, g, x, gate, ka, vta, kb, vtb, ctx, sink, w_out_bf, gb, sub, exp_lead):
    b, n, _ = x.shape
    tq = TOKEN_TILE
    n_ctx = ctx[0].shape[1]
    tok_map = lambda i, j: (i, j, 0)
    request = lambda i, j: (i, 0, 0)
    k_spec = pl.BlockSpec((1, n, KV_W), request)
    vt_spec = pl.BlockSpec((1, KV_W, n), request)
    ctx_spec = pl.BlockSpec((1, n_ctx, KV_W), request)
    return pl.pallas_call(
        functools.partial(_attn_kernel, n_ctx=n_ctx, windowed=True, n_lat=n, tq=tq, gb=gb,
                          sub=sub, own_keys=False, exp_lead=exp_lead,
                          fill_ahead=LATENT_FILL_AHEAD),
        grid=(b, n // (sub * tq)),
        in_specs=[
            pl.BlockSpec((1, N_HEADS, sub * tq, LANES), lambda i, j: (i, 0, j, 0)),
            pl.BlockSpec((1, sub * tq, MIX_WIDTH), tok_map),
            pl.BlockSpec((1, sub * tq, D_MODEL), tok_map),
            pl.BlockSpec((1, 1, D_MODEL), request),
            k_spec, vt_spec, k_spec, vt_spec,
            ctx_spec, ctx_spec, ctx_spec, ctx_spec,
            pl.BlockSpec(memory_space=pltpu.SMEM),
            pl.BlockSpec((MIX_WIDTH, D_MODEL), lambda i, j: (0, 0)),
        ],
        out_specs=pl.BlockSpec((1, sub * tq, D_MODEL), tok_map),
        out_shape=jax.ShapeDtypeStruct((b, n, D_MODEL), F32),
        scratch_shapes=[
            pltpu.VMEM((sub, MIX_WIDTH, tq), F32),
        ],
        compiler_params=pltpu.CompilerParams(
            dimension_semantics=("arbitrary", "arbitrary"), vmem_limit_bytes=VMEM_LIMIT_BYTES),
        name="attend_latent",
    )(q, g, x, gate, ka, vta, kb, vtb, *ctx, sink, w_out_bf)


def _ctx_kernel(*refs, tq, sub, exp_lead):
    (x_ref, shift_ref, scale_ref, gate_ref, gain_ref, w_ref, hg_ref, ones_ref, sink_ref, wout_ref,
     out_ref, ka32_ref, va32_ref, kb32_ref, vb32_ref,
     q_scr, g_scr, ka_scr, vta_scr, kb_scr, vtb_scr, o_scr) = refs
    _proj_kernel(x_ref, shift_ref, scale_ref, gain_ref, w_ref, hg_ref, ones_ref,
                 q_scr, g_scr, ka_scr, vta_scr, kb_scr, vtb_scr,
                 ka32_ref, va32_ref, kb32_ref, vb32_ref, rope=False, emit_f32=True)
    _attn_kernel(q_scr, g_scr, x_ref, gate_ref, ka_scr, vta_scr, kb_scr, vtb_scr,
                 sink_ref, wout_ref, out_ref, o_scr,
                 n_ctx=0, windowed=False, n_lat=tq, tq=tq, gb=GROUP, sub=sub, own_keys=True,
                 exp_lead=exp_lead, fill_ahead=CTX_FILL_AHEAD)


def _context_pass(x, shift, scale, gate, gain, w_in_bf, head_gains, ones_bd, sink, w_out_bf,
                  sub, exp_lead):
    requests, n, _ = x.shape
    tm = sub * n
    x_flat = x.reshape(1, requests * n, D_MODEL)
    const = lambda *idx: (lambda i: idx)
    tok_spec = pl.BlockSpec((1, tm, D_MODEL), lambda i: (0, i, 0))
    kv32_spec = pl.BlockSpec((sub, n, KV_W), lambda i: (i, 0, 0))
    outs = pl.pallas_call(
        functools.partial(_ctx_kernel, tq=n, sub=sub, exp_lead=exp_lead),
        grid=(requests // sub,),
        in_specs=[
            tok_spec,
            pl.BlockSpec((1, 1, D_MODEL), const(0, 0, 0)),
            pl.BlockSpec((1, 1, D_MODEL), const(0, 0, 0)),
            pl.BlockSpec((1, 1, D_MODEL), const(0, 0, 0)),
            pl.BlockSpec((1, D_MODEL), const(0, 0)),
            pl.BlockSpec((D_MODEL, IN_WIDTH), const(0, 0)),
            pl.BlockSpec((4, LANES), const(0, 0)),
            pl.BlockSpec((MXU_DIM, MXU_DIM), const(0, 0)),
            pl.BlockSpec(memory_space=pltpu.SMEM),
            pl.BlockSpec((MIX_WIDTH, D_MODEL), const(0, 0)),
        ],
        out_specs=[tok_spec] + [kv32_spec] * 4,
        out_shape=[jax.ShapeDtypeStruct((1, requests * n, D_MODEL), F32)]
        + [jax.ShapeDtypeStruct((requests, n, KV_W), F32)] * 4,
        scratch_shapes=[
            pltpu.VMEM((1, N_HEADS, tm, LANES), BF16),
            pltpu.VMEM((1, tm, MIX_WIDTH), BF16),
            pltpu.VMEM((1, tm, KV_W), BF16),
            pltpu.VMEM((1, KV_W, tm), BF16),
            pltpu.VMEM((1, tm, KV_W), BF16),
            pltpu.VMEM((1, KV_W, tm), BF16),
            pltpu.VMEM((sub, MIX_WIDTH, n), F32),
        ],
        compiler_params=pltpu.CompilerParams(
            dimension_semantics=("arbitrary",), vmem_limit_bytes=VMEM_LIMIT_BYTES),
        name="context_pass",
    )(x_flat, shift, scale, gate, gain, w_in_bf, head_gains, ones_bd, sink, w_out_bf)
    return (outs[0].reshape(requests, n, D_MODEL),) + tuple(outs[1:])


def _rope_tables(n_tokens):
    f32 = np.float32
    rows = n_tokens // GRID_W
    row = np.repeat(np.arange(rows, dtype=f32), GRID_W)
    col = np.tile(np.arange(GRID_W, dtype=f32), rows)
    n_freq = HEAD_DIM // 4
    inv = f32(ROPE_THETA) ** (-np.arange(n_freq, dtype=f32) / f32(n_freq))
    ar = row[:, None] * inv[None, :]
    ac = col[:, None] * inv[None, :]
    ang = np.concatenate([ar, ar, ac, ac], axis=-1).astype(f32)
    sign = np.tile(np.repeat(f32([-1.0, 1.0]), ROPE_QUARTER), 2)
    cos = np.tile(np.cos(ang), (1, LANES // HEAD_DIM))
    sin_signed = np.tile(np.sin(ang) * sign[None, :], (1, LANES // HEAD_DIM))
    return jnp.asarray(cos, F32), jnp.asarray(sin_signed, F32)


def _ones_blockdiag():
    idx = np.arange(MXU_DIM) // HEAD_DIM
    return jnp.asarray(idx[:, None] == idx[None, :], BF16)


def kernel(x_prompt, x_sample, cache_k_a, cache_v_a, cache_k_b, cache_v_b, c, c_ctx,
           w_mod, b_mod, norm_gain, w_in, qn_a, kn_a, qn_b, kn_b, sink_b, w_out):
    depth = w_in.shape[0]
    batch, seq, _ = x_prompt.shape
    dec_batch, dec_seq, _ = x_sample.shape
    past = cache_k_a.shape[2]

    rope_tables = _rope_tables(dec_seq)
    ones_bd = _ones_blockdiag()
    n_cond = 1 + dec_batch
    cond_rows = -(-n_cond // 8) * 8
    cond = jnp.concatenate(
        [c_ctx[None, :], c, jnp.zeros((cond_rows - n_cond, D_MODEL), F32)], axis=0)

    xp, xs = x_prompt, x_sample
    new_kv = [[], [], [], []]
    tile2 = lambda v: jnp.tile(v, LANES // HEAD_DIM)
    for l in range(depth):
        w_in_bf = w_in[l].astype(BF16)
        w_out_bf = w_out[l].astype(BF16)
        q_scale = HEAD_DIM ** -0.5 * LOG2E
        head_gains = jnp.stack([tile2(qn_a[l]) * q_scale, tile2(kn_a[l]),
                                tile2(qn_b[l]) * q_scale, tile2(kn_b[l])])
        gain = norm_gain[l].reshape(1, D_MODEL)
        sink = sink_b[l]

        m = _modulation(cond, w_mod[l], b_mod[l])
        shift, scale, gate = (m[:, i * D_MODEL:(i + 1) * D_MODEL] for i in range(3))

        sel = lambda v: v[0:1].reshape(1, 1, D_MODEL)
        xp, ka32, va32, kb32, vb32 = _context_pass(
            xp, sel(shift), sel(scale), sel(gate), gain, w_in_bf, head_gains, ones_bd, sink,
            w_out_bf, CTX_TILES_PER_STEP, CTX_EXP_LEAD)
        for acc, v in zip(new_kv, (ka32, va32, kb32, vb32)):
            acc.append(v.reshape(batch, seq, N_KV_A, HEAD_DIM))

        sel = lambda v: v[1:n_cond].reshape(dec_batch, 1, D_MODEL)
        q, g, ka, vta, kb, vtb = _project_latent(
            xs, sel(shift), sel(scale), gain, w_in_bf, head_gains, ones_bd, rope_tables)
        ctx = tuple(cache[:, l].reshape(dec_batch, past, KV_W)
                    for cache in (cache_k_a, cache_v_a, cache_k_b, cache_v_b))
        xs = _attend_latent(q, g, xs, sel(gate), ka, vta, kb, vtb, ctx, sink, w_out_bf,
                            LATENT_HEADS_PER_TASK, LATENT_TILES_PER_STEP, LATENT_EXP_LEAD)

    return (xp, xs) + tuple(jnp.stack(v, axis=1) for v in new_kv)
```

```python
import functools

import numpy as np
import jax
import jax.numpy as jnp
from jax import lax
from jax.experimental import pallas as pl
from jax.experimental.pallas import tpu as pltpu

F32 = jnp.float32
BF16 = jnp.bfloat16

D_MODEL = 1024
HEAD_DIM = 64
N_HEADS_A = 8
N_KV_A = 2
N_HEADS_B = 8
N_HEADS = N_HEADS_A + N_HEADS_B
GROUP = N_HEADS_A // N_KV_A
WIDTH_A = N_HEADS_A * HEAD_DIM
WIDTH_B = N_HEADS_B * HEAD_DIM
MIX_WIDTH = WIDTH_A + WIDTH_B
KV_W = N_KV_A * HEAD_DIM
IN_WIDTH = 2 * (2 * WIDTH_A + 2 * KV_W)
GRID_W = 64
WINDOW = 128
ROPE_THETA = 10000.0
EPS = 1e-6
NEG_INF = -1e30
LOG2E = 1.4426950408889634

ROPE_QUARTER = HEAD_DIM // 4

LANES = 128
SUBLANES = 8
MXU_DIM = 256
BF16_ROWS = 16
VMEM_LIMIT_BYTES = 56 * 1024 * 1024

_OFF_QA = 0
_OFF_KA = _OFF_QA + WIDTH_A
_OFF_VA = _OFF_KA + KV_W
_OFF_GA = _OFF_VA + KV_W
_OFF_QB = _OFF_GA + WIDTH_A
_OFF_KB = _OFF_QB + WIDTH_B
_OFF_VB = _OFF_KB + KV_W
_OFF_GB = _OFF_VB + KV_W

TOKEN_TILE = 256
PROJ_TILE = 1024
MOD_COLS = 768
KEY_CHUNK = 256
CTX_TILES_PER_STEP = 4
CTX_EXP_LEAD = None
LATENT_TILES_PER_STEP = 2
LATENT_HEADS_PER_TASK = 2
LATENT_EXP_LEAD = 2
LATENT_FILL_AHEAD = 4
CTX_FILL_AHEAD = 3


def _dot(a, b):
    return jnp.dot(a, b, preferred_element_type=F32)


def _dot_t(a, b):
    return lax.dot_general(a, b, (((1,), (1,)), ((), ())), preferred_element_type=F32)


def _order_after(x, token):
    zero = (pltpu.bitcast(token, jnp.uint32) >> 16) >> 16
    bits = pltpu.bitcast(x, jnp.uint32)
    zero = jnp.concatenate([zero[0:1, :]] * (bits.shape[1] // LANES), axis=1)
    return pltpu.bitcast(bits | jnp.broadcast_to(zero, bits.shape), x.dtype)


def _mod_kernel(cond_ref, w_ref, b_ref, out_ref):
    c = cond_ref[...]
    s = c * jax.nn.sigmoid(c)
    out_ref[...] = _dot(s.astype(BF16), w_ref[...].astype(BF16)) + b_ref[...]


def _modulation(cond, w_mod, b_mod):
    rows = cond.shape[0]
    n_out = w_mod.shape[1]
    bn = MOD_COLS
    return pl.pallas_call(
        _mod_kernel,
        grid=(n_out // bn,),
        in_specs=[
            pl.BlockSpec((rows, D_MODEL), lambda j: (0, 0)),
            pl.BlockSpec((D_MODEL, bn), lambda j: (0, j)),
            pl.BlockSpec((1, bn), lambda j: (0, j)),
        ],
        out_specs=pl.BlockSpec((rows, bn), lambda j: (0, j)),
        out_shape=jax.ShapeDtypeStruct((rows, n_out), F32),
        compiler_params=pltpu.CompilerParams(
            dimension_semantics=("arbitrary",), vmem_limit_bytes=VMEM_LIMIT_BYTES),
        name="modulation",
    )(cond, w_mod, b_mod.reshape(1, n_out))


def _head_rms(blk, ones_blockdiag):
    ss = _dot((blk * blk).astype(BF16), ones_blockdiag)
    return blk * lax.rsqrt(ss * (1.0 / HEAD_DIM) + EPS)


def _rope(blk, cos, sin_signed, low_half):
    up = pltpu.roll(blk, LANES - ROPE_QUARTER, 1)
    down = pltpu.roll(blk, ROPE_QUARTER, 1)
    return blk * cos + jnp.where(low_half, up, down) * sin_signed


def _proj_kernel(*refs, rope, emit_f32):
    it = iter(refs)
    x_ref, shift_ref, scale_ref, gain_ref, w_ref, hg_ref, ones_ref = (next(it) for _ in range(7))
    if rope:
        cos_ref, sin_ref = next(it), next(it)
    q_ref, g_ref, ka_ref, vta_ref, kb_ref, vtb_ref = (next(it) for _ in range(6))
    if emit_f32:
        ka32_ref, va32_ref, kb32_ref, vb32_ref = (next(it) for _ in range(4))

    x = x_ref[0]
    ms = jnp.mean(x * x, axis=-1, keepdims=True)
    h = x * lax.rsqrt(ms + EPS) * gain_ref[...]
    h = h * (1.0 + scale_ref[0]) + shift_ref[0]
    hb = h.astype(BF16)

    lane = lax.broadcasted_iota(jnp.int32, (1, LANES), 1)
    if rope:
        cos = cos_ref[...]
        sin_signed = sin_ref[...]
        low_half = (lane % (2 * ROPE_QUARTER)) < ROPE_QUARTER

    ones256 = ones_ref[...]

    def seg(off, width):
        return _dot(hb, w_ref[:, off:off + width])

    def normed_chunks(p, gains):
        out = []
        for c0 in range(0, p.shape[1], MXU_DIM):
            y = _head_rms(p[:, c0:c0 + MXU_DIM], ones256)
            for c1 in range(0, MXU_DIM, LANES):
                yc = y[:, c1:c1 + LANES] * gains[(c0 + c1) // LANES]
                if rope:
                    yc = _rope(yc, cos, sin_signed, low_half)
                out.append(yc)
        return out

    def store_q(chunks, head0):
        for hh in range(2 * len(chunks)):
            kv = hh // GROUP
            c = chunks[hh // 2]
            if hh % 2 != kv:
                c = pltpu.roll(c, HEAD_DIM, 1)
            keep = (lane < HEAD_DIM) if kv == 0 else (lane >= HEAD_DIM)
            q_ref[0, head0 + hh] = jnp.where(keep, c, 0.0).astype(BF16)

    def silu(v):
        return v * jax.nn.sigmoid(v)

    p_qa = seg(_OFF_QA, WIDTH_A)
    p_qb = seg(_OFF_QB, WIDTH_B)
    qa = normed_chunks(p_qa, [hg_ref[0:1, :]] * (WIDTH_A // LANES))
    p_kva = seg(_OFF_KA, 2 * KV_W)
    store_q(qa, 0)
    qb = normed_chunks(p_qb, [hg_ref[2:3, :]] * (WIDTH_B // LANES))
    p_kvb = seg(_OFF_KB, 2 * KV_W)
    store_q(qb, N_HEADS_A)
    k_both = jnp.concatenate([p_kva[:, 0:KV_W], p_kvb[:, 0:KV_W]], axis=1)
    ka, kb = normed_chunks(k_both, [hg_ref[1:2, :], hg_ref[3:4, :]])
    p_ga = seg(_OFF_GA, WIDTH_A)
    p_gb = seg(_OFF_GB, WIDTH_B)
    ka_ref[0] = ka.astype(BF16)
    kb_ref[0] = kb.astype(BF16)
    va = p_kva[:, KV_W:2 * KV_W]
    vb = p_kvb[:, KV_W:2 * KV_W]
    vta_ref[0] = va.T.astype(BF16)
    vtb_ref[0] = vb.T.astype(BF16)
    g_ref[0, :, 0:WIDTH_A] = silu(p_ga).astype(BF16)
    g_ref[0, :, WIDTH_A:MIX_WIDTH] = silu(p_gb).astype(BF16)
    if emit_f32:
        for ref, val in ((ka32_ref, ka), (va32_ref, va), (kb32_ref, kb), (vb32_ref, vb)):
            ref[...] = val.reshape(ref.shape)


def _project_latent(x, shift, scale, gain, w_in_bf, head_gains, ones_bd, rope_tables):
    b, n, _ = x.shape
    tm = PROJ_TILE
    request = lambda i, j: (i, 0, 0)
    const = lambda i, j: (0, 0)
    k_spec = pl.BlockSpec((1, tm, KV_W), lambda i, j: (i, j, 0))
    vt_spec = pl.BlockSpec((1, KV_W, tm), lambda i, j: (i, 0, j))
    k_shape = jax.ShapeDtypeStruct((b, n, KV_W), BF16)
    vt_shape = jax.ShapeDtypeStruct((b, KV_W, n), BF16)
    return pl.pallas_call(
        functools.partial(_proj_kernel, rope=True, emit_f32=False),
        grid=(b, n // tm),
        in_specs=[
            pl.BlockSpec((1, tm, D_MODEL), lambda i, j: (i, j, 0)),
            pl.BlockSpec((1, 1, D_MODEL), request),
            pl.BlockSpec((1, 1, D_MODEL), request),
            pl.BlockSpec((1, D_MODEL), const),
            pl.BlockSpec((D_MODEL, IN_WIDTH), const),
            pl.BlockSpec((4, LANES), const),
            pl.BlockSpec((MXU_DIM, MXU_DIM), const),
            pl.BlockSpec((tm, LANES), lambda i, j: (j, 0)),
            pl.BlockSpec((tm, LANES), lambda i, j: (j, 0)),
        ],
        out_specs=[
            pl.BlockSpec((1, N_HEADS, tm, LANES), lambda i, j: (i, 0, j, 0)),
            pl.BlockSpec((1, tm, MIX_WIDTH), lambda i, j: (i, j, 0)),
            k_spec, vt_spec, k_spec, vt_spec,
        ],
        out_shape=[
            jax.ShapeDtypeStruct((b, N_HEADS, n, LANES), BF16),
            jax.ShapeDtypeStruct((b, n, MIX_WIDTH), BF16),
            k_shape, vt_shape, k_shape, vt_shape,
        ],
        compiler_params=pltpu.CompilerParams(
            dimension_semantics=("arbitrary", "arbitrary"), vmem_limit_bytes=VMEM_LIMIT_BYTES),
        name="project_latent",
    )(x, shift, scale, gain, w_in_bf, head_gains, ones_bd, *rope_tables)


def _attn_kernel(*refs, n_ctx, windowed, n_lat, tq, gb, sub, own_keys, exp_lead, fill_ahead):
    it = iter(refs)
    q_ref, g_ref, x_ref, gate_ref, ka_ref, vta_ref, kb_ref, vtb_ref = (next(it) for _ in range(8))
    if n_ctx:
        cka_ref, cva_ref, ckb_ref, cvb_ref = (next(it) for _ in range(4))
    sink_ref, wout_ref, out_ref = (next(it) for _ in range(3))
    o_scr = next(it)

    width = gb * tq

    if n_ctx:
        ctx_k = {"a": cka_ref[0].astype(BF16), "b": ckb_ref[0].astype(BF16)}
        ctx_vt = {"a": cva_ref[0].T.astype(BF16), "b": cvb_ref[0].T.astype(BF16)}

    band = tq + 2 * WINDOW

    def band_of(u):
        t = pl.program_id(1) * sub + u
        start = jnp.clip(t * tq - WINDOW, 0, n_lat - band)
        start = pl.multiple_of(start, WINDOW)
        kpos = start + lax.broadcasted_iota(jnp.int32, (band, tq), 0)
        qpos = t * tq + lax.broadcasted_iota(jnp.int32, (band, tq), 1)
        bias = jnp.where(jnp.abs(kpos - qpos) <= WINDOW, 0.0, NEG_INF)
        return start, jnp.concatenate([bias] * gb, axis=1)

    ck = KEY_CHUNK

    def spans(total):
        return [(r, min(ck, total - r)) for r in range(0, total, ck)]

    def chunks(u, mixer, kv):
        rows = slice(kv * HEAD_DIM, (kv + 1) * HEAD_DIM)
        k0 = u * n_lat if own_keys else 0
        out = []
        for r, n in spans(n_ctx):
            out.append((n,
                        functools.partial(lambda r, n: ctx_k[mixer][r:r + n, :], r, n),
                        functools.partial(lambda r, n: ctx_vt[mixer][rows, r:r + n], r, n), None))
        if mixer == "a":
            for r, n in spans(n_lat):
                r += k0
                out.append((n,
                            functools.partial(lambda r, n: ka_ref[0, r:r + n, :], r, n),
                            functools.partial(lambda r, n: vta_ref[0, rows, r:r + n], r, n), None))
        elif windowed:
            start, bias = band_of(u)
            for r, n in spans(band):
                out.append((n,
                            functools.partial(
                                lambda r, n: kb_ref[0, pl.ds(k0 + start + r, n), :], r, n),
                            functools.partial(
                                lambda r, n: vtb_ref[0, rows, pl.ds(k0 + start + r, n)], r, n),
                            bias[r:r + n, :]))
        else:
            for r, n in spans(n_lat):
                r += k0
                out.append((n,
                            functools.partial(lambda r, n: kb_ref[0, r:r + n, :], r, n),
                            functools.partial(lambda r, n: vtb_ref[0, rows, r:r + n], r, n), None))
        return out

    def sink_row(h0):
        return jnp.concatenate(
            [jnp.full((1, tq), sink_ref[h0 - N_HEADS_A + j] * LOG2E, F32) for j in range(gb)],
            axis=1)

    tasks = ([("a", h0) for h0 in range(0, N_HEADS_A, gb)]
             + [("b", N_HEADS_A + h0) for h0 in range(0, N_HEADS_B, gb)])
    items = []
    for u in range(sub):
        for ti, (mixer, h0) in enumerate(tasks):
            todo = chunks(u, mixer, (h0 % N_HEADS_A) // GROUP)
            for idx, chunk in enumerate(todo):
                last = idx == len(todo) - 1
                items.append((u, mixer, h0, chunk, idx == 0, last, last and ti == len(tasks) - 1))

    scores = {}
    running = {}
    tokens = []

    def emit_scores(k):
        u, mixer, h0, (_, load_k, _, kbias), _, _, _ = items[k]
        qg = q_ref[0, h0:h0 + gb, u * tq:(u + 1) * tq, :].reshape(width, LANES)
        s = _dot_t(load_k(), qg)
        scores[k] = s if kbias is None else s + kbias

    def emit_merge(u):
        toks = slice(u * tq, (u + 1) * tq)
        o = o_scr[u].T
        gated = (o * g_ref[0, toks, :].astype(F32)).astype(BF16)
        y = _dot(gated, wout_ref[...])
        out_ref[0, toks, :] = x_ref[0, toks, :] + gate_ref[0] * y

    def emit_softmax(k):
        u, mixer, h0, (nk, _, load_vt, _), first, last, tile_done = items[k]
        s = scores.pop(k)
        m_new = jnp.max(s, axis=0, keepdims=True)
        if first:
            if mixer == "b":
                m_new = jnp.maximum(m_new, sink_row(h0))
        else:
            m_old, o_old = running.pop((u, h0))
            m_new = jnp.maximum(m_old, m_new)
        if exp_lead is not None and k >= exp_lead:
            m_new = _order_after(m_new, tokens[k - exp_lead])
        p = jnp.exp2(s - m_new)
        vt_ones = jnp.concatenate([load_vt(), jnp.ones((BF16_ROWS, nk), BF16)], axis=0)
        o = _dot(vt_ones, p.astype(BF16))
        if not first:
            o = o_old * jnp.exp2(m_old - m_new) + o
        tokens.append(o[0:SUBLANES, 0:LANES])
        if not last:
            running[(u, h0)] = (m_new, o)
            return
        l = o[HEAD_DIM:HEAD_DIM + 1, :]
        if mixer == "b":
            l = l + jnp.exp2(sink_row(h0) - m_new)
        o = o[0:HEAD_DIM, :] * (1.0 / l)
        for j in range(gb):
            o_scr[u, (h0 + j) * HEAD_DIM:(h0 + j + 1) * HEAD_DIM, :] = o[:, j * tq:(j + 1) * tq]
        if tile_done:
            emit_merge(u)

    for k in range(len(items) + fill_ahead):
        if k < len(items):
            emit_scores(k)
        if k >= fill_ahead:
            emit_softmax(k - fill_ahead)


def _attend_latent(q, g, x, gate, ka, vta, kb, vtb, ctx, sink, w_out_bf, gb, sub, exp_lead):
    b, n, _ = x.shape
    tq = TOKEN_TILE
    n_ctx = ctx[0].shape[1]
    tok_map = lambda i, j: (i, j, 0)
    request = lambda i, j: (i, 0, 0)
    k_spec = pl.BlockSpec((1, n, KV_W), request)
    vt_spec = pl.BlockSpec((1, KV_W, n), request)
    ctx_spec = pl.BlockSpec((1, n_ctx, KV_W), request)
    return pl.pallas_call(
        functools.partial(_attn_kernel, n_ctx=n_ctx, windowed=True, n_lat=n, tq=tq, gb=gb,
                          sub=sub, own_keys=False, exp_lead=exp_lead,
                          fill_ahead=LATENT_FILL_AHEAD),
        grid=(b, n // (sub * tq)),
        in_specs=[
            pl.BlockSpec((1, N_HEADS, sub * tq, LANES), lambda i, j: (i, 0, j, 0)),
            pl.BlockSpec((1, sub * tq, MIX_WIDTH), tok_map),
            pl.BlockSpec((1, sub * tq, D_MODEL), tok_map),
            pl.BlockSpec((1, 1, D_MODEL), request),
            k_spec, vt_spec, k_spec, vt_spec,
            ctx_spec, ctx_spec, ctx_spec, ctx_spec,
            pl.BlockSpec(memory_space=pltpu.SMEM),
            pl.BlockSpec((MIX_WIDTH, D_MODEL), lambda i, j: (0, 0)),
        ],
        out_specs=pl.BlockSpec((1, sub * tq, D_MODEL), tok_map),
        out_shape=jax.ShapeDtypeStruct((b, n, D_MODEL), F32),
        scratch_shapes=[
            pltpu.VMEM((sub, MIX_WIDTH, tq), F32),
        ],
        compiler_params=pltpu.CompilerParams(
            dimension_semantics=("arbitrary", "arbitrary"), vmem_limit_bytes=VMEM_LIMIT_BYTES),
        name="attend_latent",
    )(q, g, x, gate, ka, vta, kb, vtb, *ctx, sink, w_out_bf)


def _ctx_kernel(*refs, tq, sub, exp_lead):
    (x_ref, shift_ref, scale_ref, gate_ref, gain_ref, w_ref, hg_ref, ones_ref, sink_ref, wout_ref,
     out_ref, ka32_ref, va32_ref, kb32_ref, vb32_ref,
     q_scr, g_scr, ka_scr, vta_scr, kb_scr, vtb_scr, o_scr) = refs
    _proj_kernel(x_ref, shift_ref, scale_ref, gain_ref, w_ref, hg_ref, ones_ref,
                 q_scr, g_scr, ka_scr, vta_scr, kb_scr, vtb_scr,
                 ka32_ref, va32_ref, kb32_ref, vb32_ref, rope=False, emit_f32=True)
    _attn_kernel(q_scr, g_scr, x_ref, gate_ref, ka_scr, vta_scr, kb_scr, vtb_scr,
                 sink_ref, wout_ref, out_ref, o_scr,
                 n_ctx=0, windowed=False, n_lat=tq, tq=tq, gb=GROUP, sub=sub, own_keys=True,
                 exp_lead=exp_lead, fill_ahead=CTX_FILL_AHEAD)


def _context_pass(x, shift, scale, gate, gain, w_in_bf, head_gains, ones_bd, sink, w_out_bf,
                  sub, exp_lead):
    requests, n, _ = x.shape
    tm = sub * n
    x_flat = x.reshape(1, requests * n, D_MODEL)
    const = lambda *idx: (lambda i: idx)
    tok_spec = pl.BlockSpec((1, tm, D_MODEL), lambda i: (0, i, 0))
    kv32_spec = pl.BlockSpec((sub, n, KV_W), lambda i: (i, 0, 0))
    outs = pl.pallas_call(
        functools.partial(_ctx_kernel, tq=n, sub=sub, exp_lead=exp_lead),
        grid=(requests // sub,),
        in_specs=[
            tok_spec,
            pl.BlockSpec((1, 1, D_MODEL), const(0, 0, 0)),
            pl.BlockSpec((1, 1, D_MODEL), const(0, 0, 0)),
            pl.BlockSpec((1, 1, D_MODEL), const(0, 0, 0)),
            pl.BlockSpec((1, D_MODEL), const(0, 0)),
            pl.BlockSpec((D_MODEL, IN_WIDTH), const(0, 0)),
            pl.BlockSpec((4, LANES), const(0, 0)),
            pl.BlockSpec((MXU_DIM, MXU_DIM), const(0, 0)),
            pl.BlockSpec(memory_space=pltpu.SMEM),
            pl.BlockSpec((MIX_WIDTH, D_MODEL), const(0, 0)),
        ],
        out_specs=[tok_spec] + [kv32_spec] * 4,
        out_shape=[jax.ShapeDtypeStruct((1, requests * n, D_MODEL), F32)]
        + [jax.ShapeDtypeStruct((requests, n, KV_W), F32)] * 4,
        scratch_shapes=[
            pltpu.VMEM((1, N_HEADS, tm, LANES), BF16),
            pltpu.VMEM((1, tm, MIX_WIDTH), BF16),
            pltpu.VMEM((1, tm, KV_W), BF16),
            pltpu.VMEM((1, KV_W, tm), BF16),
            pltpu.VMEM((1, tm, KV_W), BF16),
            pltpu.VMEM((1, KV_W, tm), BF16),
            pltpu.VMEM((sub, MIX_WIDTH, n), F32),
        ],
        compiler_params=pltpu.CompilerParams(
            dimension_semantics=("arbitrary",), vmem_limit_bytes=VMEM_LIMIT_BYTES),
        name="context_pass",
    )(x_flat, shift, scale, gate, gain, w_in_bf, head_gains, ones_bd, sink, w_out_bf)
    return (outs[0].reshape(requests, n, D_MODEL),) + tuple(outs[1:])


def _rope_tables(n_tokens):
    f32 = np.float32
    rows = n_tokens // GRID_W
    row = np.repeat(np.arange(rows, dtype=f32), GRID_W)
    col = np.tile(np.arange(GRID_W, dtype=f32), rows)
    n_freq = HEAD_DIM // 4
    inv = f32(ROPE_THETA) ** (-np.arange(n_freq, dtype=f32) / f32(n_freq))
    ar = row[:, None] * inv[None, :]
    ac = col[:, None] * inv[None, :]
    ang = np.concatenate([ar, ar, ac, ac], axis=-1).astype(f32)
    sign = np.tile(np.repeat(f32([-1.0, 1.0]), ROPE_QUARTER), 2)
    cos = np.tile(np.cos(ang), (1, LANES // HEAD_DIM))
    sin_signed = np.tile(np.sin(ang) * sign[None, :], (1, LANES // HEAD_DIM))
    return jnp.asarray(cos, F32), jnp.asarray(sin_signed, F32)


def _ones_blockdiag():
    idx = np.arange(MXU_DIM) // HEAD_DIM
    return jnp.asarray(idx[:, None] == idx[None, :], BF16)


def kernel(x_prompt, x_sample, cache_k_a, cache_v_a, cache_k_b, cache_v_b, c, c_ctx,
           w_mod, b_mod, norm_gain, w_in, qn_a, kn_a, qn_b, kn_b, sink_b, w_out):
    depth = w_in.shape[0]
    batch, seq, _ = x_prompt.shape
    dec_batch, dec_seq, _ = x_sample.shape
    past = cache_k_a.shape[2]

    rope_tables = _rope_tables(dec_seq)
    ones_bd = _ones_blockdiag()
    n_cond = 1 + dec_batch
    cond_rows = -(-n_cond // 8) * 8
    cond = jnp.concatenate(
        [c_ctx[None, :], c, jnp.zeros((cond_rows - n_cond, D_MODEL), F32)], axis=0)

    xp, xs = x_prompt, x_sample
    new_kv = [[], [], [], []]
    tile2 = lambda v: jnp.tile(v, LANES // HEAD_DIM)
    for l in range(depth):
        w_in_bf = w_in[l].astype(BF16)
        w_out_bf = w_out[l].astype(BF16)
        q_scale = HEAD_DIM ** -0.5 * LOG2E
        head_gains = jnp.stack([tile2(qn_a[l]) * q_scale, tile2(kn_a[l]),
                                tile2(qn_b[l]) * q_scale, tile2(kn_b[l])])
        gain = norm_gain[l].reshape(1, D_MODEL)
        sink = sink_b[l]

        m = _modulation(cond, w_mod[l], b_mod[l])
        shift, scale, gate = (m[:, i * D_MODEL:(i + 1) * D_MODEL] for i in range(3))

        sel = lambda v: v[0:1].reshape(1, 1, D_MODEL)
        xp, ka32, va32, kb32, vb32 = _context_pass(
            xp, sel(shift), sel(scale), sel(gate), gain, w_in_bf, head_gains, ones_bd, sink,
            w_out_bf, CTX_TILES_PER_STEP, CTX_EXP_LEAD)
        for acc, v in zip(new_kv, (ka32, va32, kb32, vb32)):
            acc.append(v.reshape(batch, seq, N_KV_A, HEAD_DIM))

        sel = lambda v: v[1:n_cond].reshape(dec_batch, 1, D_MODEL)
        q, g, ka, vta, kb, vtb = _project_latent(
            xs, sel(shift), sel(scale), gain, w_in_bf, head_gains, ones_bd, rope_tables)
        ctx = tuple(cache[:, l].reshape(dec_batch, past, KV_W)
                    for cache in (cache_k_a, cache_v_a, cache_k_b, cache_v_b))
        xs = _attend_latent(q, g, xs, sel(gate), ka, vta, kb, vtb, ctx, sink, w_out_bf,
                            LATENT_HEADS_PER_TASK, LATENT_TILES_PER_STEP, LATENT_EXP_LEAD)

    return (xp, xs) + tuple(jnp.stack(v, axis=1) for v in new_kv)
```

```python
import functools

import numpy as np
import jax
import jax.numpy as jnp
from jax import lax
from jax.experimental import pallas as pl
from jax.experimental.pallas import tpu as pltpu

F32 = jnp.float32
BF16 = jnp.bfloat16

D_MODEL = 1024
HEAD_DIM = 64
N_HEADS_A = 8
N_KV_A = 2
N_HEADS_B = 8
N_HEADS = N_HEADS_A + N_HEADS_B
GROUP = N_HEADS_A // N_KV_A
WIDTH_A = N_HEADS_A * HEAD_DIM
WIDTH_B = N_HEADS_B * HEAD_DIM
MIX_WIDTH = WIDTH_A + WIDTH_B
KV_W = N_KV_A * HEAD_DIM
IN_WIDTH = 2 * (2 * WIDTH_A + 2 * KV_W)
GRID_W = 64
WINDOW = 128
ROPE_THETA = 10000.0
EPS = 1e-6
NEG_INF = -1e30
LOG2E = 1.4426950408889634

ROPE_QUARTER = HEAD_DIM // 4

LANES = 128
SUBLANES = 8
MXU_DIM = 256
BF16_ROWS = 16
VMEM_LIMIT_BYTES = 56 * 1024 * 1024

_OFF_QA = 0
_OFF_KA = _OFF_QA + WIDTH_A
_OFF_VA = _OFF_KA + KV_W
_OFF_GA = _OFF_VA + KV_W
_OFF_QB = _OFF_GA + WIDTH_A
_OFF_KB = _OFF_QB + WIDTH_B
_OFF_VB = _OFF_KB + KV_W
_OFF_GB = _OFF_VB + KV_W

TOKEN_TILE = 256
PROJ_TILE = 1024
MOD_COLS = 768
KEY_CHUNK = 256
CTX_TILES_PER_STEP = 4
CTX_EXP_LEAD = None
LATENT_TILES_PER_STEP = 2
LATENT_HEADS_PER_TASK = 2
LATENT_EXP_LEAD = 2
LATENT_FILL_AHEAD = 3
CTX_FILL_AHEAD = 3


def _dot(a, b):
    return jnp.dot(a, b, preferred_element_type=F32)


def _dot_t(a, b):
    return lax.dot_general(a, b, (((1,), (1,)), ((), ())), preferred_element_type=F32)


def _order_after(x, token):
    zero = (pltpu.bitcast(token, jnp.uint32) >> 16) >> 16
    bits = pltpu.bitcast(x, jnp.uint32)
    zero = jnp.concatenate([zero[0:1, :]] * (bits.shape[1] // LANES), axis=1)
    return pltpu.bitcast(bits | jnp.broadcast_to(zero, bits.shape), x.dtype)


def _mod_kernel(cond_ref, w_ref, b_ref, out_ref):
    c = cond_ref[...]
    s = c * jax.nn.sigmoid(c)
    out_ref[...] = _dot(s.astype(BF16), w_ref[...].astype(BF16)) + b_ref[...]


def _modulation(cond, w_mod, b_mod):
    rows = cond.shape[0]
    n_out = w_mod.shape[1]
    bn = MOD_COLS
    return pl.pallas_call(
        _mod_kernel,
        grid=(n_out // bn,),
        in_specs=[
            pl.BlockSpec((rows, D_MODEL), lambda j: (0, 0)),
            pl.BlockSpec((D_MODEL, bn), lambda j: (0, j)),
            pl.BlockSpec((1, bn), lambda j: (0, j)),
        ],
        out_specs=pl.BlockSpec((rows, bn), lambda j: (0, j)),
        out_shape=jax.ShapeDtypeStruct((rows, n_out), F32),
        compiler_params=pltpu.CompilerParams(
            dimension_semantics=("arbitrary",), vmem_limit_bytes=VMEM_LIMIT_BYTES),
        name="modulation",
    )(cond, w_mod, b_mod.reshape(1, n_out))


def _head_rms(blk, ones_blockdiag):
    ss = _dot((blk * blk).astype(BF16), ones_blockdiag)
    return blk * lax.rsqrt(ss * (1.0 / HEAD_DIM) + EPS)


def _rope(blk, cos, sin_signed, low_half):
    up = pltpu.roll(blk, LANES - ROPE_QUARTER, 1)
    down = pltpu.roll(blk, ROPE_QUARTER, 1)
    return blk * cos + jnp.where(low_half, up, down) * sin_signed


def _proj_kernel(*refs, rope, emit_f32):
    it = iter(refs)
    x_ref, shift_ref, scale_ref, gain_ref, w_ref, hg_ref, ones_ref = (next(it) for _ in range(7))
    if rope:
        cos_ref, sin_ref = next(it), next(it)
    q_ref, g_ref, ka_ref, vta_ref, kb_ref, vtb_ref = (next(it) for _ in range(6))
    if emit_f32:
        ka32_ref, va32_ref, kb32_ref, vb32_ref = (next(it) for _ in range(4))

    x = x_ref[0]
    ms = jnp.mean(x * x, axis=-1, keepdims=True)
    h = x * lax.rsqrt(ms + EPS) * gain_ref[...]
    h = h * (1.0 + scale_ref[0]) + shift_ref[0]
    hb = h.astype(BF16)

    lane = lax.broadcasted_iota(jnp.int32, (1, LANES), 1)
    if rope:
        cos = cos_ref[...]
        sin_signed = sin_ref[...]
        low_half = (lane % (2 * ROPE_QUARTER)) < ROPE_QUARTER

    ones256 = ones_ref[...]

    def seg(off, width):
        return _dot(hb, w_ref[:, off:off + width])

    def normed_chunks(p, gains):
        out = []
        for c0 in range(0, p.shape[1], MXU_DIM):
            y = _head_rms(p[:, c0:c0 + MXU_DIM], ones256)
            for c1 in range(0, MXU_DIM, LANES):
                yc = y[:, c1:c1 + LANES] * gains[(c0 + c1) // LANES]
                if rope:
                    yc = _rope(yc, cos, sin_signed, low_half)
                out.append(yc)
        return out

    def store_q(chunks, head0):
        for hh in range(2 * len(chunks)):
            kv = hh // GROUP
            c = chunks[hh // 2]
            if hh % 2 != kv:
                c = pltpu.roll(c, HEAD_DIM, 1)
            keep = (lane < HEAD_DIM) if kv == 0 else (lane >= HEAD_DIM)
            q_ref[0, head0 + hh] = jnp.where(keep, c, 0.0).astype(BF16)

    def silu(v):
        return v * jax.nn.sigmoid(v)

    p_qa = seg(_OFF_QA, WIDTH_A)
    p_qb = seg(_OFF_QB, WIDTH_B)
    qa = normed_chunks(p_qa, [hg_ref[0:1, :]] * (WIDTH_A // LANES))
    p_kva = seg(_OFF_KA, 2 * KV_W)
    store_q(qa, 0)
    qb = normed_chunks(p_qb, [hg_ref[2:3, :]] * (WIDTH_B // LANES))
    p_kvb = seg(_OFF_KB, 2 * KV_W)
    store_q(qb, N_HEADS_A)
    k_both = jnp.concatenate([p_kva[:, 0:KV_W], p_kvb[:, 0:KV_W]], axis=1)
    ka, kb = normed_chunks(k_both, [hg_ref[1:2, :], hg_ref[3:4, :]])
    p_ga = seg(_OFF_GA, WIDTH_A)
    p_gb = seg(_OFF_GB, WIDTH_B)
    ka_ref[0] = ka.astype(BF16)
    kb_ref[0] = kb.astype(BF16)
    va = p_kva[:, KV_W:2 * KV_W]
    vb = p_kvb[:, KV_W:2 * KV_W]
    vta_ref[0] = va.T.astype(BF16)
    vtb_ref[0] = vb.T.astype(BF16)
    g_ref[0, :, 0:WIDTH_A] = silu(p_ga).astype(BF16)
    g_ref[0, :, WIDTH_A:MIX_WIDTH] = silu(p_gb).astype(BF16)
    if emit_f32:
        for ref, val in ((ka32_ref, ka), (va32_ref, va), (kb32_ref, kb), (vb32_ref, vb)):
            ref[...] = val.reshape(ref.shape)


def _project_latent(x, shift, scale, gain, w_in_bf, head_gains, ones_bd, rope_tables):
    b, n, _ = x.shape
    tm = PROJ_TILE
    request = lambda i, j: (i, 0, 0)
    const = lambda i, j: (0, 0)
    k_spec = pl.BlockSpec((1, tm, KV_W), lambda i, j: (i, j, 0))
    vt_spec = pl.BlockSpec((1, KV_W, tm), lambda i, j: (i, 0, j))
    k_shape = jax.ShapeDtypeStruct((b, n, KV_W), BF16)
    vt_shape = jax.ShapeDtypeStruct((b, KV_W, n), BF16)
    return pl.pallas_call(
        functools.partial(_proj_kernel, rope=True, emit_f32=False),
        grid=(b, n // tm),
        in_specs=[
            pl.BlockSpec((1, tm, D_MODEL), lambda i, j: (i, j, 0)),
            pl.BlockSpec((1, 1, D_MODEL), request),
            pl.BlockSpec((1, 1, D_MODEL), request),
            pl.BlockSpec((1, D_MODEL), const),
            pl.BlockSpec((D_MODEL, IN_WIDTH), const),
            pl.BlockSpec((4, LANES), const),
            pl.BlockSpec((MXU_DIM, MXU_DIM), const),
            pl.BlockSpec((tm, LANES), lambda i, j: (j, 0)),
            pl.BlockSpec((tm, LANES), lambda i, j: (j, 0)),
        ],
        out_specs=[
            pl.BlockSpec((1, N_HEADS, tm, LANES), lambda i, j: (i, 0, j, 0)),
            pl.BlockSpec((1, tm, MIX_WIDTH), lambda i, j: (i, j, 0)),
            k_spec, vt_spec, k_spec, vt_spec,
        ],
        out_shape=[
            jax.ShapeDtypeStruct((b, N_HEADS, n, LANES), BF16),
            jax.ShapeDtypeStruct((b, n, MIX_WIDTH), BF16),
            k_shape, vt_shape, k_shape, vt_shape,
        ],
        compiler_params=pltpu.CompilerParams(
            dimension_semantics=("arbitrary", "arbitrary"), vmem_limit_bytes=VMEM_LIMIT_BYTES),
        name="project_latent",
    )(x, shift, scale, gain, w_in_bf, head_gains, ones_bd, *rope_tables)


def _attn_kernel(*refs, n_ctx, windowed, n_lat, tq, gb, sub, own_keys, exp_lead, fill_ahead):
    it = iter(refs)
    q_ref, g_ref, x_ref, gate_ref, ka_ref, vta_ref, kb_ref, vtb_ref = (next(it) for _ in range(8))
    if n_ctx:
        cka_ref, cva_ref, ckb_ref, cvb_ref = (next(it) for _ in range(4))
    sink_ref, wout_ref, out_ref = (next(it) for _ in range(3))
    o_scr = next(it)

    width = gb * tq

    if n_ctx:
        ctx_k = {"a": cka_ref[0].astype(BF16), "b": ckb_ref[0].astype(BF16)}
        ctx_vt = {"a": cva_ref[0].T.astype(BF16), "b": cvb_ref[0].T.astype(BF16)}

    band = tq + 2 * WINDOW

    def band_of(u):
        t = pl.program_id(1) * sub + u
        start = jnp.clip(t * tq - WINDOW, 0, n_lat - band)
        start = pl.multiple_of(start, WINDOW)
        kpos = start + lax.broadcasted_iota(jnp.int32, (band, tq), 0)
        qpos = t * tq + lax.broadcasted_iota(jnp.int32, (band, tq), 1)
        bias = jnp.where(jnp.abs(kpos - qpos) <= WINDOW, 0.0, NEG_INF)
        return start, jnp.concatenate([bias] * gb, axis=1)

    ck = KEY_CHUNK

    def spans(total):
        return [(r, min(ck, total - r)) for r in range(0, total, ck)]

    def chunks(u, mixer, kv):
        rows = slice(kv * HEAD_DIM, (kv + 1) * HEAD_DIM)
        k0 = u * n_lat if own_keys else 0
        out = []
        for r, n in spans(n_ctx):
            out.append((n,
                        functools.partial(lambda r, n: ctx_k[mixer][r:r + n, :], r, n),
                        functools.partial(lambda r, n: ctx_vt[mixer][rows, r:r + n], r, n), None))
        if mixer == "a":
            for r, n in spans(n_lat):
                r += k0
                out.append((n,
                            functools.partial(lambda r, n: ka_ref[0, r:r + n, :], r, n),
                            functools.partial(lambda r, n: vta_ref[0, rows, r:r + n], r, n), None))
        elif windowed:
            start, bias = band_of(u)
            for r, n in spans(band):
                out.append((n,
                            functools.partial(
                                lambda r, n: kb_ref[0, pl.ds(k0 + start + r, n), :], r, n),
                            functools.partial(
                                lambda r, n: vtb_ref[0, rows, pl.ds(k0 + start + r, n)], r, n),
                            bias[r:r + n, :]))
        else:
            for r, n in spans(n_lat):
                r += k0
                out.append((n,
                            functools.partial(lambda r, n: kb_ref[0, r:r + n, :], r, n),
                            functools.partial(lambda r, n: vtb_ref[0, rows, r:r + n], r, n), None))
        return out

    def sink_row(h0):
        return jnp.concatenate(
            [jnp.full((1, tq), sink_ref[h0 - N_HEADS_A + j] * LOG2E, F32) for j in range(gb)],
            axis=1)

    tasks = ([("a", h0) for h0 in range(0, N_HEADS_A, gb)]
             + [("b", N_HEADS_A + h0) for h0 in range(0, N_HEADS_B, gb)])
    items = []
    for u in range(sub):
        for ti, (mixer, h0) in enumerate(tasks):
            todo = chunks(u, mixer, (h0 % N_HEADS_A) // GROUP)
            for idx, chunk in enumerate(todo):
                last = idx == len(todo) - 1
                items.append((u, mixer, h0, chunk, idx == 0, last, last and ti == len(tasks) - 1))

    scores = {}
    running = {}
    tokens = []

    def emit_scores(k):
        u, mixer, h0, (_, load_k, _, kbias), _, _, _ = items[k]
        qg = q_ref[0, h0:h0 + gb, u * tq:(u + 1) * tq, :].reshape(width, LANES)
        s = _dot_t(load_k(), qg)
        scores[k] = s if kbias is None else s + kbias

    def emit_merge(u):
        toks = slice(u * tq, (u + 1) * tq)
        o = o_scr[u].T
        gated = (o * g_ref[0, toks, :].astype(F32)).astype(BF16)
        y = _dot(gated, wout_ref[...])
        out_ref[0, toks, :] = x_ref[0, toks, :] + gate_ref[0] * y

    def emit_softmax(k):
        u, mixer, h0, (nk, _, load_vt, _), first, last, tile_done = items[k]
        s = scores.pop(k)
        m_new = jnp.max(s, axis=0, keepdims=True)
        if first:
            if mixer == "b":
                m_new = jnp.maximum(m_new, sink_row(h0))
        else:
            m_old, o_old = running.pop((u, h0))
            m_new = jnp.maximum(m_old, m_new)
        if exp_lead is not None and k >= exp_lead:
            m_new = _order_after(m_new, tokens[k - exp_lead])
        p = jnp.exp2(s - m_new)
        vt_ones = jnp.concatenate([load_vt(), jnp.ones((BF16_ROWS, nk), BF16)], axis=0)
        o = _dot(vt_ones, p.astype(BF16))
        tokens.append(o[0:SUBLANES, 0:LANES])
        if not first:
            o = o_old * jnp.exp2(m_old - m_new) + o
        if not last:
            running[(u, h0)] = (m_new, o)
            return
        l = o[HEAD_DIM:HEAD_DIM + 1, :]
        if mixer == "b":
            l = l + jnp.exp2(sink_row(h0) - m_new)
        o = o[0:HEAD_DIM, :] * (1.0 / l)
        for j in range(gb):
            o_scr[u, (h0 + j) * HEAD_DIM:(h0 + j + 1) * HEAD_DIM, :] = o[:, j * tq:(j + 1) * tq]
        if tile_done:
            emit_merge(u)

    for k in range(len(items) + fill_ahead):
        if k < len(items):
            emit_scores(k)
        if k >= fill_ahead:
            emit_softmax(k - fill_ahead)


def _attend_latent(q, g, x, gate, ka, vta, kb, vtb, ctx, sink, w_out_bf, gb, sub, exp_lead):
    b, n, _ = x.shape
    tq = TOKEN_TILE
    n_ctx = ctx[0].shape[1]
    tok_map = lambda i, j: (i, j, 0)
    request = lambda i, j: (i, 0, 0)
    k_spec = pl.BlockSpec((1, n, KV_W), request)
    vt_spec = pl.BlockSpec((1, KV_W, n), request)
    ctx_spec = pl.BlockSpec((1, n_ctx, KV_W), request)
    return pl.pallas_call(
        functools.partial(_attn_kernel, n_ctx=n_ctx, windowed=True, n_lat=n, tq=tq, gb=gb,
                          sub=sub, own_keys=False, exp_lead=exp_lead,
                          fill_ahead=LATENT_FILL_AHEAD),
        grid=(b, n // (sub * tq)),
        in_specs=[
            pl.BlockSpec((1, N_HEADS, sub * tq, LANES), lambda i, j: (i, 0, j, 0)),
            pl.BlockSpec((1, sub * tq, MIX_WIDTH), tok_map),
            pl.BlockSpec((1, sub * tq, D_MODEL), tok_map),
            pl.BlockSpec((1, 1, D_MODEL), request),
            k_spec, vt_spec, k_spec, vt_spec,
            ctx_spec, ctx_spec, ctx_spec, ctx_spec,
            pl.BlockSpec(memory_space=pltpu.SMEM),
            pl.BlockSpec((MIX_WIDTH, D_MODEL), lambda i, j: (0, 0)),
        ],
        out_specs=pl.BlockSpec((1, sub * tq, D_MODEL), tok_map),
        out_shape=jax.ShapeDtypeStruct((b, n, D_MODEL), F32),
        scratch_shapes=[
            pltpu.VMEM((sub, MIX_WIDTH, tq), F32),
        ],
        compiler_params=pltpu.CompilerParams(
            dimension_semantics=("arbitrary", "arbitrary"), vmem_limit_bytes=VMEM_LIMIT_BYTES),
        name="attend_latent",
    )(q, g, x, gate, ka, vta, kb, vtb, *ctx, sink, w_out_bf)


def _ctx_kernel(*refs, tq, sub, exp_lead):
    (x_ref, shift_ref, scale_ref, gate_ref, gain_ref, w_ref, hg_ref, ones_ref, sink_ref, wout_ref,
     out_ref, ka32_ref, va32_ref, kb32_ref, vb32_ref,
     q_scr, g_scr, ka_scr, vta_scr, kb_scr, vtb_scr, o_scr) = refs
    _proj_kernel(x_ref, shift_ref, scale_ref, gain_ref, w_ref, hg_ref, ones_ref,
                 q_scr, g_scr, ka_scr, vta_scr, kb_scr, vtb_scr,
                 ka32_ref, va32_ref, kb32_ref, vb32_ref, rope=False, emit_f32=True)
    _attn_kernel(q_scr, g_scr, x_ref, gate_ref, ka_scr, vta_scr, kb_scr, vtb_scr,
                 sink_ref, wout_ref, out_ref, o_scr,
                 n_ctx=0, windowed=False, n_lat=tq, tq=tq, gb=GROUP, sub=sub, own_keys=True,
                 exp_lead=exp_lead, fill_ahead=CTX_FILL_AHEAD)


def _context_pass(x, shift, scale, gate, gain, w_in_bf, head_gains, ones_bd, sink, w_out_bf,
                  sub, exp_lead):
    requests, n, _ = x.shape
    tm = sub * n
    x_flat = x.reshape(1, requests * n, D_MODEL)
    const = lambda *idx: (lambda i: idx)
    tok_spec = pl.BlockSpec((1, tm, D_MODEL), lambda i: (0, i, 0))
    kv32_spec = pl.BlockSpec((sub, n, KV_W), lambda i: (i, 0, 0))
    outs = pl.pallas_call(
        functools.partial(_ctx_kernel, tq=n, sub=sub, exp_lead=exp_lead),
        grid=(requests // sub,),
        in_specs=[
            tok_spec,
            pl.BlockSpec((1, 1, D_MODEL), const(0, 0, 0)),
            pl.BlockSpec((1, 1, D_MODEL), const(0, 0, 0)),
            pl.BlockSpec((1, 1, D_MODEL), const(0, 0, 0)),
            pl.BlockSpec((1, D_MODEL), const(0, 0)),
            pl.BlockSpec((D_MODEL, IN_WIDTH), const(0, 0)),
            pl.BlockSpec((4, LANES), const(0, 0)),
            pl.BlockSpec((MXU_DIM, MXU_DIM), const(0, 0)),
            pl.BlockSpec(memory_space=pltpu.SMEM),
            pl.BlockSpec((MIX_WIDTH, D_MODEL), const(0, 0)),
        ],
        out_specs=[tok_spec] + [kv32_spec] * 4,
        out_shape=[jax.ShapeDtypeStruct((1, requests * n, D_MODEL), F32)]
        + [jax.ShapeDtypeStruct((requests, n, KV_W), F32)] * 4,
        scratch_shapes=[
            pltpu.VMEM((1, N_HEADS, tm, LANES), BF16),
            pltpu.VMEM((1, tm, MIX_WIDTH), BF16),
            pltpu.VMEM((1, tm, KV_W), BF16),
            pltpu.VMEM((1, KV_W, tm), BF16),
            pltpu.VMEM((1, tm, KV_W), BF16),
            pltpu.VMEM((1, KV_W, tm), BF16),
            pltpu.VMEM((sub, MIX_WIDTH, n), F32),
        ],
        compiler_params=pltpu.CompilerParams(
            dimension_semantics=("arbitrary",), vmem_limit_bytes=VMEM_LIMIT_BYTES),
        name="context_pass",
    )(x_flat, shift, scale, gate, gain, w_in_bf, head_gains, ones_bd, sink, w_out_bf)
    return (outs[0].reshape(requests, n, D_MODEL),) + tuple(outs[1:])


def _rope_tables(n_tokens):
    f32 = np.float32
    rows = n_tokens // GRID_W
    row = np.repeat(np.arange(rows, dtype=f32), GRID_W)
    col = np.tile(np.arange(GRID_W, dtype=f32), rows)
    n_freq = HEAD_DIM // 4
    inv = f32(ROPE_THETA) ** (-np.arange(n_freq, dtype=f32) / f32(n_freq))
    ar = row[:, None] * inv[None, :]
    ac = col[:, None] * inv[None, :]
    ang = np.concatenate([ar, ar, ac, ac], axis=-1).astype(f32)
    sign = np.tile(np.repeat(f32([-1.0, 1.0]), ROPE_QUARTER), 2)
    cos = np.tile(np.cos(ang), (1, LANES // HEAD_DIM))
    sin_signed = np.tile(np.sin(ang) * sign[None, :], (1, LANES // HEAD_DIM))
    return jnp.asarray(cos, F32), jnp.asarray(sin_signed, F32)


def _ones_blockdiag():
    idx = np.arange(MXU_DIM) // HEAD_DIM
    return jnp.asarray(idx[:, None] == idx[None, :], BF16)


def kernel(x_prompt, x_sample, cache_k_a, cache_v_a, cache_k_b, cache_v_b, c, c_ctx,
           w_mod, b_mod, norm_gain, w_in, qn_a, kn_a, qn_b, kn_b, sink_b, w_out):
    depth = w_in.shape[0]
    batch, seq, _ = x_prompt.shape
    dec_batch, dec_seq, _ = x_sample.shape
    past = cache_k_a.shape[2]

    rope_tables = _rope_tables(dec_seq)
    ones_bd = _ones_blockdiag()
    n_cond = 1 + dec_batch
    cond_rows = -(-n_cond // 8) * 8
    cond = jnp.concatenate(
        [c_ctx[None, :], c, jnp.zeros((cond_rows - n_cond, D_MODEL), F32)], axis=0)

    xp, xs = x_prompt, x_sample
    new_kv = [[], [], [], []]
    tile2 = lambda v: jnp.tile(v, LANES // HEAD_DIM)
    for l in range(depth):
        w_in_bf = w_in[l].astype(BF16)
        w_out_bf = w_out[l].astype(BF16)
        q_scale = HEAD_DIM ** -0.5 * LOG2E
        head_gains = jnp.stack([tile2(qn_a[l]) * q_scale, tile2(kn_a[l]),
                                tile2(qn_b[l]) * q_scale, tile2(kn_b[l])])
        gain = norm_gain[l].reshape(1, D_MODEL)
        sink = sink_b[l]

        m = _modulation(cond, w_mod[l], b_mod[l])
        shift, scale, gate = (m[:, i * D_MODEL:(i + 1) * D_MODEL] for i in range(3))

        sel = lambda v: v[0:1].reshape(1, 1, D_MODEL)
        xp, ka32, va32, kb32, vb32 = _context_pass(
            xp, sel(shift), sel(scale), sel(gate), gain, w_in_bf, head_gains, ones_bd, sink,
            w_out_bf, CTX_TILES_PER_STEP, CTX_EXP_LEAD)
        for acc, v in zip(new_kv, (ka32, va32, kb32, vb32)):
            acc.append(v.reshape(batch, seq, N_KV_A, HEAD_DIM))

        sel = lambda v: v[1:n_cond].reshape(dec_batch, 1, D_MODEL)
        q, g, ka, vta, kb, vtb = _project_latent(
            xs, sel(shift), sel(scale), gain, w_in_bf, head_gains, ones_bd, rope_tables)
        ctx = tuple(cache[:, l].reshape(dec_batch, past, KV_W)
                    for cache in (cache_k_a, cache_v_a, cache_k_b, cache_v_b))
        xs = _attend_latent(q, g, xs, sel(gate), ka, vta, kb, vtb, ctx, sink, w_out_bf,
                            LATENT_HEADS_PER_TASK, LATENT_TILES_PER_STEP, LATENT_EXP_LEAD)

    return (xp, xs) + tuple(jnp.stack(v, axis=1) for v in new_kv)
```

```python
import functools

import numpy as np
import jax
import jax.numpy as jnp
from jax import lax
from jax.experimental import pallas as pl
from jax.experimental.pallas import tpu as pltpu

F32 = jnp.float32
BF16 = jnp.bfloat16

D_MODEL = 1024
HEAD_DIM = 64
N_HEADS_A = 8
N_KV_A = 2
N_HEADS_B = 8
N_HEADS = N_HEADS_A + N_HEADS_B
GROUP = N_HEADS_A // N_KV_A
WIDTH_A = N_HEADS_A * HEAD_DIM
WIDTH_B = N_HEADS_B * HEAD_DIM
MIX_WIDTH = WIDTH_A + WIDTH_B
KV_W = N_KV_A * HEAD_DIM
IN_WIDTH = 2 * (2 * WIDTH_A + 2 * KV_W)
GRID_W = 64
WINDOW = 128
ROPE_THETA = 10000.0
EPS = 1e-6
NEG_INF = -1e30
LOG2E = 1.4426950408889634

ROPE_QUARTER = HEAD_DIM // 4

LANES = 128
SUBLANES = 8
MXU_DIM = 256
BF16_ROWS = 16
VMEM_LIMIT_BYTES = 56 * 1024 * 1024

_OFF_QA = 0
_OFF_KA = _OFF_QA + WIDTH_A
_OFF_VA = _OFF_KA + KV_W
_OFF_GA = _OFF_VA + KV_W
_OFF_QB = _OFF_GA + WIDTH_A
_OFF_KB = _OFF_QB + WIDTH_B
_OFF_VB = _OFF_KB + KV_W
_OFF_GB = _OFF_VB + KV_W

TOKEN_TILE = 256
PROJ_TILE = 1024
MOD_COLS = 1536
KEY_CHUNK = 256
CTX_TILES_PER_STEP = 4
CTX_EXP_LEAD = None
LATENT_TILES_PER_STEP = 2
LATENT_HEADS_PER_TASK = 2
LATENT_EXP_LEAD = 2
LATENT_FILL_AHEAD = 3
CTX_FILL_AHEAD = 3


def _dot(a, b):
    return jnp.dot(a, b, preferred_element_type=F32)


def _dot_t(a, b):
    return lax.dot_general(a, b, (((1,), (1,)), ((), ())), preferred_element_type=F32)


def _order_after(x, token):
    zero = (pltpu.bitcast(token, jnp.uint32) >> 16) >> 16
    bits = pltpu.bitcast(x, jnp.uint32)
    zero = jnp.concatenate([zero[0:1, :]] * (bits.shape[1] // LANES), axis=1)
    return pltpu.bitcast(bits | jnp.broadcast_to(zero, bits.shape), x.dtype)


def _mod_kernel(cond_ref, w_ref, b_ref, out_ref):
    c = cond_ref[...]
    s = c * jax.nn.sigmoid(c)
    out_ref[...] = _dot(s.astype(BF16), w_ref[...].astype(BF16)) + b_ref[...]


def _modulation(cond, w_mod, b_mod):
    rows = cond.shape[0]
    n_out = w_mod.shape[1]
    bn = MOD_COLS
    return pl.pallas_call(
        _mod_kernel,
        grid=(n_out // bn,),
        in_specs=[
            pl.BlockSpec((rows, D_MODEL), lambda j: (0, 0)),
            pl.BlockSpec((D_MODEL, bn), lambda j: (0, j)),
            pl.BlockSpec((1, bn), lambda j: (0, j)),
        ],
        out_specs=pl.BlockSpec((rows, bn), lambda j: (0, j)),
        out_shape=jax.ShapeDtypeStruct((rows, n_out), F32),
        compiler_params=pltpu.CompilerParams(
            dimension_semantics=("arbitrary",), vmem_limit_bytes=VMEM_LIMIT_BYTES),
        name="modulation",
    )(cond, w_mod, b_mod.reshape(1, n_out))


def _head_rms(blk, ones_blockdiag):
    ss = _dot((blk * blk).astype(BF16), ones_blockdiag)
    return blk * lax.rsqrt(ss * (1.0 / HEAD_DIM) + EPS)


def _rope(blk, cos, sin_signed, low_half):
    up = pltpu.roll(blk, LANES - ROPE_QUARTER, 1)
    down = pltpu.roll(blk, ROPE_QUARTER, 1)
    return blk * cos + jnp.where(low_half, up, down) * sin_signed


def _proj_kernel(*refs, rope, emit_f32):
    it = iter(refs)
    x_ref, shift_ref, scale_ref, gain_ref, w_ref, hg_ref, ones_ref = (next(it) for _ in range(7))
    if rope:
        cos_ref, sin_ref = next(it), next(it)
    q_ref, g_ref, ka_ref, vta_ref, kb_ref, vtb_ref = (next(it) for _ in range(6))
    if emit_f32:
        ka32_ref, va32_ref, kb32_ref, vb32_ref = (next(it) for _ in range(4))

    x = x_ref[0]
    ms = jnp.mean(x * x, axis=-1, keepdims=True)
    h = x * lax.rsqrt(ms + EPS) * gain_ref[...]
    h = h * (1.0 + scale_ref[0]) + shift_ref[0]
    hb = h.astype(BF16)

    lane = lax.broadcasted_iota(jnp.int32, (1, LANES), 1)
    if rope:
        cos = cos_ref[...]
        sin_signed = sin_ref[...]
        low_half = (lane % (2 * ROPE_QUARTER)) < ROPE_QUARTER

    ones256 = ones_ref[...]

    def seg(off, width):
        return _dot(hb, w_ref[:, off:off + width])

    def normed_chunks(p, gains):
        out = []
        for c0 in range(0, p.shape[1], MXU_DIM):
            y = _head_rms(p[:, c0:c0 + MXU_DIM], ones256)
            for c1 in range(0, MXU_DIM, LANES):
                yc = y[:, c1:c1 + LANES] * gains[(c0 + c1) // LANES]
                if rope:
                    yc = _rope(yc, cos, sin_signed, low_half)
                out.append(yc)
        return out

    def store_q(chunks, head0):
        for hh in range(2 * len(chunks)):
            kv = hh // GROUP
            c = chunks[hh // 2]
            if hh % 2 != kv:
                c = pltpu.roll(c, HEAD_DIM, 1)
            keep = (lane < HEAD_DIM) if kv == 0 else (lane >= HEAD_DIM)
            q_ref[0, head0 + hh] = jnp.where(keep, c, 0.0).astype(BF16)

    def silu(v):
        return v * jax.nn.sigmoid(v)

    p_qa = seg(_OFF_QA, WIDTH_A)
    p_qb = seg(_OFF_QB, WIDTH_B)
    qa = normed_chunks(p_qa, [hg_ref[0:1, :]] * (WIDTH_A // LANES))
    p_kva = seg(_OFF_KA, 2 * KV_W)
    store_q(qa, 0)
    qb = normed_chunks(p_qb, [hg_ref[2:3, :]] * (WIDTH_B // LANES))
    p_kvb = seg(_OFF_KB, 2 * KV_W)
    store_q(qb, N_HEADS_A)
    k_both = jnp.concatenate([p_kva[:, 0:KV_W], p_kvb[:, 0:KV_W]], axis=1)
    ka, kb = normed_chunks(k_both, [hg_ref[1:2, :], hg_ref[3:4, :]])
    p_ga = seg(_OFF_GA, WIDTH_A)
    p_gb = seg(_OFF_GB, WIDTH_B)
    ka_ref[0] = ka.astype(BF16)
    kb_ref[0] = kb.astype(BF16)
    va = p_kva[:, KV_W:2 * KV_W]
    vb = p_kvb[:, KV_W:2 * KV_W]
    vta_ref[0] = va.T.astype(BF16)
    vtb_ref[0] = vb.T.astype(BF16)
    g_ref[0, :, 0:WIDTH_A] = silu(p_ga).astype(BF16)
    g_ref[0, :, WIDTH_A:MIX_WIDTH] = silu(p_gb).astype(BF16)
    if emit_f32:
        for ref, val in ((ka32_ref, ka), (va32_ref, va), (kb32_ref, kb), (vb32_ref, vb)):
            ref[...] = val.reshape(ref.shape)


def _project_latent(x, shift, scale, gain, w_in_bf, head_gains, ones_bd, rope_tables):
    b, n, _ = x.shape
    tm = PROJ_TILE
    request = lambda i, j: (i, 0, 0)
    const = lambda i, j: (0, 0)
    k_spec = pl.BlockSpec((1, tm, KV_W), lambda i, j: (i, j, 0))
    vt_spec = pl.BlockSpec((1, KV_W, tm), lambda i, j: (i, 0, j))
    k_shape = jax.ShapeDtypeStruct((b, n, KV_W), BF16)
    vt_shape = jax.ShapeDtypeStruct((b, KV_W, n), BF16)
    return pl.pallas_call(
        functools.partial(_proj_kernel, rope=True, emit_f32=False),
        grid=(b, n // tm),
        in_specs=[
            pl.BlockSpec((1, tm, D_MODEL), lambda i, j: (i, j, 0)),
            pl.BlockSpec((1, 1, D_MODEL), request),
            pl.BlockSpec((1, 1, D_MODEL), request),
            pl.BlockSpec((1, D_MODEL), const),
            pl.BlockSpec((D_MODEL, IN_WIDTH), const),
            pl.BlockSpec((4, LANES), const),
            pl.BlockSpec((MXU_DIM, MXU_DIM), const),
            pl.BlockSpec((tm, LANES), lambda i, j: (j, 0)),
            pl.BlockSpec((tm, LANES), lambda i, j: (j, 0)),
        ],
        out_specs=[
            pl.BlockSpec((1, N_HEADS, tm, LANES), lambda i, j: (i, 0, j, 0)),
            pl.BlockSpec((1, tm, MIX_WIDTH), lambda i, j: (i, j, 0)),
            k_spec, vt_spec, k_spec, vt_spec,
        ],
        out_shape=[
            jax.ShapeDtypeStruct((b, N_HEADS, n, LANES), BF16),
            jax.ShapeDtypeStruct((b, n, MIX_WIDTH), BF16),
            k_shape, vt_shape, k_shape, vt_shape,
        ],
        compiler_params=pltpu.CompilerParams(
            dimension_semantics=("arbitrary", "arbitrary"), vmem_limit_bytes=VMEM_LIMIT_BYTES),
        name="project_latent",
    )(x, shift, scale, gain, w_in_bf, head_gains, ones_bd, *rope_tables)


def _attn_kernel(*refs, n_ctx, windowed, n_lat, tq, gb, sub, own_keys, exp_lead, fill_ahead):
    it = iter(refs)
    q_ref, g_ref, x_ref, gate_ref, ka_ref, vta_ref, kb_ref, vtb_ref = (next(it) for _ in range(8))
    if n_ctx:
        cka_ref, cva_ref, ckb_ref, cvb_ref = (next(it) for _ in range(4))
    sink_ref, wout_ref, out_ref = (next(it) for _ in range(3))
    o_scr = next(it)

    width = gb * tq

    if n_ctx:
        ctx_k = {"a": cka_ref[0].astype(BF16), "b": ckb_ref[0].astype(BF16)}
        ctx_vt = {"a": cva_ref[0].T.astype(BF16), "b": cvb_ref[0].T.astype(BF16)}

    band = tq + 2 * WINDOW

    def band_of(u):
        t = pl.program_id(1) * sub + u
        start = jnp.clip(t * tq - WINDOW, 0, n_lat - band)
        start = pl.multiple_of(start, WINDOW)
        kpos = start + lax.broadcasted_iota(jnp.int32, (band, tq), 0)
        qpos = t * tq + lax.broadcasted_iota(jnp.int32, (band, tq), 1)
        bias = jnp.where(jnp.abs(kpos - qpos) <= WINDOW, 0.0, NEG_INF)
        return start, jnp.concatenate([bias] * gb, axis=1)

    ck = KEY_CHUNK

    def spans(total):
        return [(r, min(ck, total - r)) for r in range(0, total, ck)]

    def chunks(u, mixer, kv):
        rows = slice(kv * HEAD_DIM, (kv + 1) * HEAD_DIM)
        k0 = u * n_lat if own_keys else 0
        out = []
        for r, n in spans(n_ctx):
            out.append((n,
                        functools.partial(lambda r, n: ctx_k[mixer][r:r + n, :], r, n),
                        functools.partial(lambda r, n: ctx_vt[mixer][rows, r:r + n], r, n), None))
        if mixer == "a":
            for r, n in spans(n_lat):
                r += k0
                out.append((n,
                            functools.partial(lambda r, n: ka_ref[0, r:r + n, :], r, n),
                            functools.partial(lambda r, n: vta_ref[0, rows, r:r + n], r, n), None))
        elif windowed:
            start, bias = band_of(u)
            for r, n in spans(band):
                out.append((n,
                            functools.partial(
                                lambda r, n: kb_ref[0, pl.ds(k0 + start + r, n), :], r, n),
                            functools.partial(
                                lambda r, n: vtb_ref[0, rows, pl.ds(k0 + start + r, n)], r, n),
                            bias[r:r + n, :]))
        else:
            for r, n in spans(n_lat):
                r += k0
                out.append((n,
                            functools.partial(lambda r, n: kb_ref[0, r:r + n, :], r, n),
                            functools.partial(lambda r, n: vtb_ref[0, rows, r:r + n], r, n), None))
        return out

    def sink_row(h0):
        return jnp.concatenate(
            [jnp.full((1, tq), sink_ref[h0 - N_HEADS_A + j] * LOG2E, F32) for j in range(gb)],
            axis=1)

    tasks = ([("a", h0) for h0 in range(0, N_HEADS_A, gb)]
             + [("b", N_HEADS_A + h0) for h0 in range(0, N_HEADS_B, gb)])
    items = []
    for u in range(sub):
        for ti, (mixer, h0) in enumerate(tasks):
            todo = chunks(u, mixer, (h0 % N_HEADS_A) // GROUP)
            for idx, chunk in enumerate(todo):
                last = idx == len(todo) - 1
                items.append((u, mixer, h0, chunk, idx == 0, last, last and ti == len(tasks) - 1))

    scores = {}
    running = {}
    tokens = []

    def emit_scores(k):
        u, mixer, h0, (_, load_k, _, kbias), _, _, _ = items[k]
        qg = q_ref[0, h0:h0 + gb, u * tq:(u + 1) * tq, :].reshape(width, LANES)
        s = _dot_t(load_k(), qg)
        scores[k] = s if kbias is None else s + kbias

    def emit_merge(u):
        toks = slice(u * tq, (u + 1) * tq)
        o = o_scr[u].T
        gated = (o * g_ref[0, toks, :].astype(F32)).astype(BF16)
        y = _dot(gated, wout_ref[...])
        out_ref[0, toks, :] = x_ref[0, toks, :] + gate_ref[0] * y

    def emit_softmax(k):
        u, mixer, h0, (nk, _, load_vt, _), first, last, tile_done = items[k]
        s = scores.pop(k)
        m_new = jnp.max(s, axis=0, keepdims=True)
        if first:
            if mixer == "b":
                m_new = jnp.maximum(m_new, sink_row(h0))
        else:
            m_old, o_old = running.pop((u, h0))
            m_new = jnp.maximum(m_old, m_new)
        if exp_lead is not None and k >= exp_lead:
            m_new = _order_after(m_new, tokens[k - exp_lead])
        p = jnp.exp2(s - m_new)
        vt_ones = jnp.concatenate([load_vt(), jnp.ones((BF16_ROWS, nk), BF16)], axis=0)
        o = _dot(vt_ones, p.astype(BF16))
        if not first:
            o = o_old * jnp.exp2(m_old - m_new) + o
        tokens.append(o[0:SUBLANES, 0:LANES])
        if not last:
            running[(u, h0)] = (m_new, o)
            return
        l = o[HEAD_DIM:HEAD_DIM + 1, :]
        if mixer == "b":
            l = l + jnp.exp2(sink_row(h0) - m_new)
        o = o[0:HEAD_DIM, :] * (1.0 / l)
        for j in range(gb):
            o_scr[u, (h0 + j) * HEAD_DIM:(h0 + j + 1) * HEAD_DIM, :] = o[:, j * tq:(j + 1) * tq]
        if tile_done:
            emit_merge(u)

    for k in range(len(items) + fill_ahead):
        if k < len(items):
            emit_scores(k)
        if k >= fill_ahead:
            emit_softmax(k - fill_ahead)


def _attend_latent(q, g, x, gate, ka, vta, kb, vtb, ctx, sink, w_out_bf, gb, sub, exp_lead):
    b, n, _ = x.shape
    tq = TOKEN_TILE
    n_ctx = ctx[0].shape[1]
    tok_map = lambda i, j: (i, j, 0)
    request = lambda i, j: (i, 0, 0)
    k_spec = pl.BlockSpec((1, n, KV_W), request)
    vt_spec = pl.BlockSpec((1, KV_W, n), request)
    ctx_spec = pl.BlockSpec((1, n_ctx, KV_W), request)
    return pl.pallas_call(
        functools.partial(_attn_kernel, n_ctx=n_ctx, windowed=True, n_lat=n, tq=tq, gb=gb,
                          sub=sub, own_keys=False, exp_lead=exp_lead,
                          fill_ahead=LATENT_FILL_AHEAD),
        grid=(b, n // (sub * tq)),
        in_specs=[
            pl.BlockSpec((1, N_HEADS, sub * tq, LANES), lambda i, j: (i, 0, j, 0)),
            pl.BlockSpec((1, sub * tq, MIX_WIDTH), tok_map),
            pl.BlockSpec((1, sub * tq, D_MODEL), tok_map),
            pl.BlockSpec((1, 1, D_MODEL), request),
            k_spec, vt_spec, k_spec, vt_spec,
            ctx_spec, ctx_spec, ctx_spec, ctx_spec,
            pl.BlockSpec(memory_space=pltpu.SMEM),
            pl.BlockSpec((MIX_WIDTH, D_MODEL), lambda i, j: (0, 0)),
        ],
        out_specs=pl.BlockSpec((1, sub * tq, D_MODEL), tok_map),
        out_shape=jax.ShapeDtypeStruct((b, n, D_MODEL), F32),
        scratch_shapes=[
            pltpu.VMEM((sub, MIX_WIDTH, tq), F32),
        ],
        compiler_params=pltpu.CompilerParams(
            dimension_semantics=("arbitrary", "arbitrary"), vmem_limit_bytes=VMEM_LIMIT_BYTES),
        name="attend_latent",
    )(q, g, x, gate, ka, vta, kb, vtb, *ctx, sink, w_out_bf)


def _ctx_kernel(*refs, tq, sub, exp_lead):
    (x_ref, shift_ref, scale_ref, gate_ref, gain_ref, w_ref, hg_ref, ones_ref, sink_ref, wout_ref,
     out_ref, ka32_ref, va32_ref, kb32_ref, vb32_ref,
     q_scr, g_scr, ka_scr, vta_scr, kb_scr, vtb_scr, o_scr) = refs
    _proj_kernel(x_ref, shift_ref, scale_ref, gain_ref, w_ref, hg_ref, ones_ref,
                 q_scr, g_scr, ka_scr, vta_scr, kb_scr, vtb_scr,
                 ka32_ref, va32_ref, kb32_ref, vb32_ref, rope=False, emit_f32=True)
    _attn_kernel(q_scr, g_scr, x_ref, gate_ref, ka_scr, vta_scr, kb_scr, vtb_scr,
                 sink_ref, wout_ref, out_ref, o_scr,
                 n_ctx=0, windowed=False, n_lat=tq, tq=tq, gb=GROUP, sub=sub, own_keys=True,
                 exp_lead=exp_lead, fill_ahead=CTX_FILL_AHEAD)


def _context_pass(x, shift, scale, gate, gain, w_in_bf, head_gains, ones_bd, sink, w_out_bf,
                  sub, exp_lead):
    requests, n, _ = x.shape
    tm = sub * n
    x_flat = x.reshape(1, requests * n, D_MODEL)
    const = lambda *idx: (lambda i: idx)
    tok_spec = pl.BlockSpec((1, tm, D_MODEL), lambda i: (0, i, 0))
    kv32_spec = pl.BlockSpec((sub, n, KV_W), lambda i: (i, 0, 0))
    outs = pl.pallas_call(
        functools.partial(_ctx_kernel, tq=n, sub=sub, exp_lead=exp_lead),
        grid=(requests // sub,),
        in_specs=[
            tok_spec,
            pl.BlockSpec((1, 1, D_MODEL), const(0, 0, 0)),
            pl.BlockSpec((1, 1, D_MODEL), const(0, 0, 0)),
            pl.BlockSpec((1, 1, D_MODEL), const(0, 0, 0)),
            pl.BlockSpec((1, D_MODEL), const(0, 0)),
            pl.BlockSpec((D_MODEL, IN_WIDTH), const(0, 0)),
            pl.BlockSpec((4, LANES), const(0, 0)),
            pl.BlockSpec((MXU_DIM, MXU_DIM), const(0, 0)),
            pl.BlockSpec(memory_space=pltpu.SMEM),
            pl.BlockSpec((MIX_WIDTH, D_MODEL), const(0, 0)),
        ],
        out_specs=[tok_spec] + [kv32_spec] * 4,
        out_shape=[jax.ShapeDtypeStruct((1, requests * n, D_MODEL), F32)]
        + [jax.ShapeDtypeStruct((requests, n, KV_W), F32)] * 4,
        scratch_shapes=[
            pltpu.VMEM((1, N_HEADS, tm, LANES), BF16),
            pltpu.VMEM((1, tm, MIX_WIDTH), BF16),
            pltpu.VMEM((1, tm, KV_W), BF16),
            pltpu.VMEM((1, KV_W, tm), BF16),
            pltpu.VMEM((1, tm, KV_W), BF16),
            pltpu.VMEM((1, KV_W, tm), BF16),
            pltpu.VMEM((sub, MIX_WIDTH, n), F32),
        ],
        compiler_params=pltpu.CompilerParams(
            dimension_semantics=("arbitrary",), vmem_limit_bytes=VMEM_LIMIT_BYTES),
        name="context_pass",
    )(x_flat, shift, scale, gate, gain, w_in_bf, head_gains, ones_bd, sink, w_out_bf)
    return (outs[0].reshape(requests, n, D_MODEL),) + tuple(outs[1:])


def _rope_tables(n_tokens):
    f32 = np.float32
    rows = n_tokens // GRID_W
    row = np.repeat(np.arange(rows, dtype=f32), GRID_W)
    col = np.tile(np.arange(GRID_W, dtype=f32), rows)
    n_freq = HEAD_DIM // 4
    inv = f32(ROPE_THETA) ** (-np.arange(n_freq, dtype=f32) / f32(n_freq))
    ar = row[:, None] * inv[None, :]
    ac = col[:, None] * inv[None, :]
    ang = np.concatenate([ar, ar, ac, ac], axis=-1).astype(f32)
    sign = np.tile(np.repeat(f32([-1.0, 1.0]), ROPE_QUARTER), 2)
    cos = np.tile(np.cos(ang), (1, LANES // HEAD_DIM))
    sin_signed = np.tile(np.sin(ang) * sign[None, :], (1, LANES // HEAD_DIM))
    return jnp.asarray(cos, F32), jnp.asarray(sin_signed, F32)


def _ones_blockdiag():
    idx = np.arange(MXU_DIM) // HEAD_DIM
    return jnp.asarray(idx[:, None] == idx[None, :], BF16)


def kernel(x_prompt, x_sample, cache_k_a, cache_v_a, cache_k_b, cache_v_b, c, c_ctx,
           w_mod, b_mod, norm_gain, w_in, qn_a, kn_a, qn_b, kn_b, sink_b, w_out):
    depth = w_in.shape[0]
    batch, seq, _ = x_prompt.shape
    dec_batch, dec_seq, _ = x_sample.shape
    past = cache_k_a.shape[2]

    rope_tables = _rope_tables(dec_seq)
    ones_bd = _ones_blockdiag()
    n_cond = 1 + dec_batch
    cond_rows = -(-n_cond // 8) * 8
    cond = jnp.concatenate(
        [c_ctx[None, :], c, jnp.zeros((cond_rows - n_cond, D_MODEL), F32)], axis=0)

    xp, xs = x_prompt, x_sample
    new_kv = [[], [], [], []]
    tile2 = lambda v: jnp.tile(v, LANES // HEAD_DIM)
    for l in range(depth):
        w_in_bf = w_in[l].astype(BF16)
        w_out_bf = w_out[l].astype(BF16)
        q_scale = HEAD_DIM ** -0.5 * LOG2E
        head_gains = jnp.stack([tile2(qn_a[l]) * q_scale, tile2(kn_a[l]),
                                tile2(qn_b[l]) * q_scale, tile2(kn_b[l])])
        gain = norm_gain[l].reshape(1, D_MODEL)
        sink = sink_b[l]

        m = _modulation(cond, w_mod[l], b_mod[l])
        shift, scale, gate = (m[:, i * D_MODEL:(i + 1) * D_MODEL] for i in range(3))

        sel = lambda v: v[0:1].reshape(1, 1, D_MODEL)
        xp, ka32, va32, kb32, vb32 = _context_pass(
            xp, sel(shift), sel(scale), sel(gate), gain, w_in_bf, head_gains, ones_bd, sink,
            w_out_bf, CTX_TILES_PER_STEP, CTX_EXP_LEAD)
        for acc, v in zip(new_kv, (ka32, va32, kb32, vb32)):
            acc.append(v.reshape(batch, seq, N_KV_A, HEAD_DIM))

        sel = lambda v: v[1:n_cond].reshape(dec_batch, 1, D_MODEL)
        q, g, ka, vta, kb, vtb = _project_latent(
            xs, sel(shift), sel(scale), gain, w_in_bf, head_gains, ones_bd, rope_tables)
        ctx = tuple(cache[:, l].reshape(dec_batch, past, KV_W)
                    for cache in (cache_k_a, cache_v_a, cache_k_b, cache_v_b))
        xs = _attend_latent(q, g, xs, sel(gate), ka, vta, kb, vtb, ctx, sink, w_out_bf,
                            LATENT_HEADS_PER_TASK, LATENT_TILES_PER_STEP, LATENT_EXP_LEAD)

    return (xp, xs) + tuple(jnp.stack(v, axis=1) for v in new_kv)
```

```python
import functools

import numpy as np
import jax
import jax.numpy as jnp
from jax import lax
from jax.experimental import pallas as pl
from jax.experimental.pallas import tpu as pltpu

F32 = jnp.float32
BF16 = jnp.bfloat16

D_MODEL = 1024
HEAD_DIM = 64
N_HEADS_A = 8
N_KV_A = 2
N_HEADS_B = 8
N_HEADS = N_HEADS_A + N_HEADS_B
GROUP = N_HEADS_A // N_KV_A
WIDTH_A = N_HEADS_A * HEAD_DIM
WIDTH_B = N_HEADS_B * HEAD_DIM
MIX_WIDTH = WIDTH_A + WIDTH_B
KV_W = N_KV_A * HEAD_DIM
IN_WIDTH = 2 * (2 * WIDTH_A + 2 * KV_W)
GRID_W = 64
WINDOW = 128
ROPE_THETA = 10000.0
EPS = 1e-6
NEG_INF = -1e30
LOG2E = 1.4426950408889634

ROPE_QUARTER = HEAD_DIM // 4

LANES = 128
SUBLANES = 8
MXU_DIM = 256
BF16_ROWS = 16
VMEM_LIMIT_BYTES = 56 * 1024 * 1024

_OFF_QA = 0
_OFF_KA = _OFF_QA + WIDTH_A
_OFF_VA = _OFF_KA + KV_W
_OFF_GA = _OFF_VA + KV_W
_OFF_QB = _OFF_GA + WIDTH_A
_OFF_KB = _OFF_QB + WIDTH_B
_OFF_VB = _OFF_KB + KV_W
_OFF_GB = _OFF_VB + KV_W

TOKEN_TILE = 256
PROJ_TILE = 1024
MOD_COLS = 3072
KEY_CHUNK = 256
CTX_TILES_PER_STEP = 4
CTX_EXP_LEAD = None
LATENT_TILES_PER_STEP = 2
LATENT_HEADS_PER_TASK = 2
LATENT_EXP_LEAD = 2
LATENT_FILL_AHEAD = 3
CTX_FILL_AHEAD = 3


def _dot(a, b):
    return jnp.dot(a, b, preferred_element_type=F32)


def _dot_t(a, b):
    return lax.dot_general(a, b, (((1,), (1,)), ((), ())), preferred_element_type=F32)


def _order_after(x, token):
    zero = (pltpu.bitcast(token, jnp.uint32) >> 16) >> 16
    bits = pltpu.bitcast(x, jnp.uint32)
    zero = jnp.concatenate([zero[0:1, :]] * (bits.shape[1] // LANES), axis=1)
    return pltpu.bitcast(bits | jnp.broadcast_to(zero, bits.shape), x.dtype)


def _mod_kernel(cond_ref, w_ref, b_ref, out_ref):
    c = cond_ref[...]
    s = c * jax.nn.sigmoid(c)
    out_ref[...] = _dot(s.astype(BF16), w_ref[...].astype(BF16)) + b_ref[...]


def _modulation(cond, w_mod, b_mod):
    rows = cond.shape[0]
    n_out = w_mod.shape[1]
    bn = MOD_COLS
    return pl.pallas_call(
        _mod_kernel,
        grid=(n_out // bn,),
        in_specs=[
            pl.BlockSpec((rows, D_MODEL), lambda j: (0, 0)),
            pl.BlockSpec((D_MODEL, bn), lambda j: (0, j)),
            pl.BlockSpec((1, bn), lambda j: (0, j)),
        ],
        out_specs=pl.BlockSpec((rows, bn), lambda j: (0, j)),
        out_shape=jax.ShapeDtypeStruct((rows, n_out), F32),
        compiler_params=pltpu.CompilerParams(
            dimension_semantics=("arbitrary",), vmem_limit_bytes=VMEM_LIMIT_BYTES),
        name="modulation",
    )(cond, w_mod, b_mod.reshape(1, n_out))


def _head_rms(blk, ones_blockdiag):
    ss = _dot((blk * blk).astype(BF16), ones_blockdiag)
    return blk * lax.rsqrt(ss * (1.0 / HEAD_DIM) + EPS)


def _rope(blk, cos, sin_signed, low_half):
    up = pltpu.roll(blk, LANES - ROPE_QUARTER, 1)
    down = pltpu.roll(blk, ROPE_QUARTER, 1)
    return blk * cos + jnp.where(low_half, up, down) * sin_signed


def _proj_kernel(*refs, rope, emit_f32):
    it = iter(refs)
    x_ref, shift_ref, scale_ref, gain_ref, w_ref, hg_ref, ones_ref = (next(it) for _ in range(7))
    if rope:
        cos_ref, sin_ref = next(it), next(it)
    q_ref, g_ref, ka_ref, vta_ref, kb_ref, vtb_ref = (next(it) for _ in range(6))
    if emit_f32:
        ka32_ref, va32_ref, kb32_ref, vb32_ref = (next(it) for _ in range(4))

    x = x_ref[0]
    ms = jnp.mean(x * x, axis=-1, keepdims=True)
    h = x * lax.rsqrt(ms + EPS) * gain_ref[...]
    h = h * (1.0 + scale_ref[0]) + shift_ref[0]
    hb = h.astype(BF16)

    lane = lax.broadcasted_iota(jnp.int32, (1, LANES), 1)
    if rope:
        cos = cos_ref[...]
        sin_signed = sin_ref[...]
        low_half = (lane % (2 * ROPE_QUARTER)) < ROPE_QUARTER

    ones256 = ones_ref[...]

    def seg(off, width):
        return _dot(hb, w_ref[:, off:off + width])

    def normed_chunks(p, gains):
        out = []
        for c0 in range(0, p.shape[1], MXU_DIM):
            y = _head_rms(p[:, c0:c0 + MXU_DIM], ones256)
            for c1 in range(0, MXU_DIM, LANES):
                yc = y[:, c1:c1 + LANES] * gains[(c0 + c1) // LANES]
                if rope:
                    yc = _rope(yc, cos, sin_signed, low_half)
                out.append(yc)
        return out

    def store_q(chunks, head0):
        for hh in range(2 * len(chunks)):
            kv = hh // GROUP
            c = chunks[hh // 2]
            if hh % 2 != kv:
                c = pltpu.roll(c, HEAD_DIM, 1)
            keep = (lane < HEAD_DIM) if kv == 0 else (lane >= HEAD_DIM)
            q_ref[0, head0 + hh] = jnp.where(keep, c, 0.0).astype(BF16)

    def silu(v):
        return v * jax.nn.sigmoid(v)

    p_qa = seg(_OFF_QA, WIDTH_A)
    p_qb = seg(_OFF_QB, WIDTH_B)
    qa = normed_chunks(p_qa, [hg_ref[0:1, :]] * (WIDTH_A // LANES))
    p_kva = seg(_OFF_KA, 2 * KV_W)
    store_q(qa, 0)
    qb = normed_chunks(p_qb, [hg_ref[2:3, :]] * (WIDTH_B // LANES))
    p_kvb = seg(_OFF_KB, 2 * KV_W)
    store_q(qb, N_HEADS_A)
    k_both = jnp.concatenate([p_kva[:, 0:KV_W], p_kvb[:, 0:KV_W]], axis=1)
    ka, kb = normed_chunks(k_both, [hg_ref[1:2, :], hg_ref[3:4, :]])
    p_ga = seg(_OFF_GA, WIDTH_A)
    p_gb = seg(_OFF_GB, WIDTH_B)
    ka_ref[0] = ka.astype(BF16)
    kb_ref[0] = kb.astype(BF16)
    va = p_kva[:, KV_W:2 * KV_W]
    vb = p_kvb[:, KV_W:2 * KV_W]
    vta_ref[0] = va.T.astype(BF16)
    vtb_ref[0] = vb.T.astype(BF16)
    g_ref[0, :, 0:WIDTH_A] = silu(p_ga).astype(BF16)
    g_ref[0, :, WIDTH_A:MIX_WIDTH] = silu(p_gb).astype(BF16)
    if emit_f32:
        for ref, val in ((ka32_ref, ka), (va32_ref, va), (kb32_ref, kb), (vb32_ref, vb)):
            ref[...] = val.reshape(ref.shape)


def _project_latent(x, shift, scale, gain, w_in_bf, head_gains, ones_bd, rope_tables):
    b, n, _ = x.shape
    tm = PROJ_TILE
    request = lambda i, j: (i, 0, 0)
    const = lambda i, j: (0, 0)
    k_spec = pl.BlockSpec((1, tm, KV_W), lambda i, j: (i, j, 0))
    vt_spec = pl.BlockSpec((1, KV_W, tm), lambda i, j: (i, 0, j))
    k_shape = jax.ShapeDtypeStruct((b, n, KV_W), BF16)
    vt_shape = jax.ShapeDtypeStruct((b, KV_W, n), BF16)
    return pl.pallas_call(
        functools.partial(_proj_kernel, rope=True, emit_f32=False),
        grid=(b, n // tm),
        in_specs=[
            pl.BlockSpec((1, tm, D_MODEL), lambda i, j: (i, j, 0)),
            pl.BlockSpec((1, 1, D_MODEL), request),
            pl.BlockSpec((1, 1, D_MODEL), request),
            pl.BlockSpec((1, D_MODEL), const),
            pl.BlockSpec((D_MODEL, IN_WIDTH), const),
            pl.BlockSpec((4, LANES), const),
            pl.BlockSpec((MXU_DIM, MXU_DIM), const),
            pl.BlockSpec((tm, LANES), lambda i, j: (j, 0)),
            pl.BlockSpec((tm, LANES), lambda i, j: (j, 0)),
        ],
        out_specs=[
            pl.BlockSpec((1, N_HEADS, tm, LANES), lambda i, j: (i, 0, j, 0)),
            pl.BlockSpec((1, tm, MIX_WIDTH), lambda i, j: (i, j, 0)),
            k_spec, vt_spec, k_spec, vt_spec,
        ],
        out_shape=[
            jax.ShapeDtypeStruct((b, N_HEADS, n, LANES), BF16),
            jax.ShapeDtypeStruct((b, n, MIX_WIDTH), BF16),
            k_shape, vt_shape, k_shape, vt_shape,
        ],
        compiler_params=pltpu.CompilerParams(
            dimension_semantics=("arbitrary", "arbitrary"), vmem_limit_bytes=VMEM_LIMIT_BYTES),
        name="project_latent",
    )(x, shift, scale, gain, w_in_bf, head_gains, ones_bd, *rope_tables)


def _attn_kernel(*refs, n_ctx, windowed, n_lat, tq, gb, sub, own_keys, exp_lead, fill_ahead):
    it = iter(refs)
    q_ref, g_ref, x_ref, gate_ref, ka_ref, vta_ref, kb_ref, vtb_ref = (next(it) for _ in range(8))
    if n_ctx:
        cka_ref, cva_ref, ckb_ref, cvb_ref = (next(it) for _ in range(4))
    sink_ref, wout_ref, out_ref = (next(it) for _ in range(3))
    o_scr = next(it)

    width = gb * tq

    if n_ctx:
        ctx_k = {"a": cka_ref[0].astype(BF16), "b": ckb_ref[0].astype(BF16)}
        ctx_vt = {"a": cva_ref[0].T.astype(BF16), "b": cvb_ref[0].T.astype(BF16)}

    band = tq + 2 * WINDOW

    def band_of(u):
        t = pl.program_id(1) * sub + u
        start = jnp.clip(t * tq - WINDOW, 0, n_lat - band)
        start = pl.multiple_of(start, WINDOW)
        kpos = start + lax.broadcasted_iota(jnp.int32, (band, tq), 0)
        qpos = t * tq + lax.broadcasted_iota(jnp.int32, (band, tq), 1)
        bias = jnp.where(jnp.abs(kpos - qpos) <= WINDOW, 0.0, NEG_INF)
        return start, jnp.concatenate([bias] * gb, axis=1)

    ck = KEY_CHUNK

    def spans(total):
        return [(r, min(ck, total - r)) for r in range(0, total, ck)]

    def chunks(u, mixer, kv):
        rows = slice(kv * HEAD_DIM, (kv + 1) * HEAD_DIM)
        k0 = u * n_lat if own_keys else 0
        out = []
        for r, n in spans(n_ctx):
            out.append((n,
                        functools.partial(lambda r, n: ctx_k[mixer][r:r + n, :], r, n),
                        functools.partial(lambda r, n: ctx_vt[mixer][rows, r:r + n], r, n), None))
        if mixer == "a":
            for r, n in spans(n_lat):
                r += k0
                out.append((n,
                            functools.partial(lambda r, n: ka_ref[0, r:r + n, :], r, n),
                            functools.partial(lambda r, n: vta_ref[0, rows, r:r + n], r, n), None))
        elif windowed:
            start, bias = band_of(u)
            for r, n in spans(band):
                out.append((n,
                            functools.partial(
                                lambda r, n: kb_ref[0, pl.ds(k0 + start + r, n), :], r, n),
                            functools.partial(
                                lambda r, n: vtb_ref[0, rows, pl.ds(k0 + start + r, n)], r, n),
                            bias[r:r + n, :]))
        else:
            for r, n in spans(n_lat):
                r += k0
                out.append((n,
                            functools.partial(lambda r, n: kb_ref[0, r:r + n, :], r, n),
                            functools.partial(lambda r, n: vtb_ref[0, rows, r:r + n], r, n), None))
        return out

    def sink_row(h0):
        return jnp.concatenate(
            [jnp.full((1, tq), sink_ref[h0 - N_HEADS_A + j] * LOG2E, F32) for j in range(gb)],
            axis=1)

    tasks = ([("a", h0) for h0 in range(0, N_HEADS_A, gb)]
             + [("b", N_HEADS_A + h0) for h0 in range(0, N_HEADS_B, gb)])
    items = []
    for u in range(sub):
        for ti, (mixer, h0) in enumerate(tasks):
            todo = chunks(u, mixer, (h0 % N_HEADS_A) // GROUP)
            for idx, chunk in enumerate(todo):
                last = idx == len(todo) - 1
                items.append((u, mixer, h0, chunk, idx == 0, last, last and ti == len(tasks) - 1))

    scores = {}
    running = {}
    tokens = []

    def emit_scores(k):
        u, mixer, h0, (_, load_k, _, kbias), _, _, _ = items[k]
        qg = q_ref[0, h0:h0 + gb, u * tq:(u + 1) * tq, :].reshape(width, LANES)
        s = _dot_t(load_k(), qg)
        scores[k] = s if kbias is None else s + kbias

    def emit_merge(u):
        toks = slice(u * tq, (u + 1) * tq)
        o = o_scr[u].T
        gated = (o * g_ref[0, toks, :].astype(F32)).astype(BF16)
        y = _dot(gated, wout_ref[...])
        out_ref[0, toks, :] = x_ref[0, toks, :] + gate_ref[0] * y

    def emit_softmax(k):
        u, mixer, h0, (nk, _, load_vt, _), first, last, tile_done = items[k]
        s = scores.pop(k)
        m_new = jnp.max(s, axis=0, keepdims=True)
        if first:
            if mixer == "b":
                m_new = jnp.maximum(m_new, sink_row(h0))
        else:
            m_old, o_old = running.pop((u, h0))
            m_new = jnp.maximum(m_old, m_new)
        if exp_lead is not None and k >= exp_lead:
            m_new = _order_after(m_new, tokens[k - exp_lead])
        p = jnp.exp2(s - m_new)
        vt_ones = jnp.concatenate([load_vt(), jnp.ones((BF16_ROWS, nk), BF16)], axis=0)
        o = _dot(vt_ones, p.astype(BF16))
        if not first:
            o = o_old * jnp.exp2(m_old - m_new) + o
        tokens.append(o[0:SUBLANES, 0:LANES])
        if not last:
            running[(u, h0)] = (m_new, o)
            return
        l = o[HEAD_DIM:HEAD_DIM + 1, :]
        if mixer == "b":
            l = l + jnp.exp2(sink_row(h0) - m_new)
        o = o[0:HEAD_DIM, :] * (1.0 / l)
        for j in range(gb):
            o_scr[u, (h0 + j) * HEAD_DIM:(h0 + j + 1) * HEAD_DIM, :] = o[:, j * tq:(j + 1) * tq]
        if tile_done:
            emit_merge(u)

    for k in range(len(items) + fill_ahead):
        if k < len(items):
            emit_scores(k)
        if k >= fill_ahead:
            emit_softmax(k - fill_ahead)


def _attend_latent(q, g, x, gate, ka, vta, kb, vtb, ctx, sink, w_out_bf, gb, sub, exp_lead):
    b, n, _ = x.shape
    tq = TOKEN_TILE
    n_ctx = ctx[0].shape[1]
    tok_map = lambda i, j: (i, j, 0)
    request = lambda i, j: (i, 0, 0)
    k_spec = pl.BlockSpec((1, n, KV_W), request)
    vt_spec = pl.BlockSpec((1, KV_W, n), request)
    ctx_spec = pl.BlockSpec((1, n_ctx, KV_W), request)
    return pl.pallas_call(
        functools.partial(_attn_kernel, n_ctx=n_ctx, windowed=True, n_lat=n, tq=tq, gb=gb,
                          sub=sub, own_keys=False, exp_lead=exp_lead,
                          fill_ahead=LATENT_FILL_AHEAD),
        grid=(b, n // (sub * tq)),
        in_specs=[
            pl.BlockSpec((1, N_HEADS, sub * tq, LANES), lambda i, j: (i, 0, j, 0)),
            pl.BlockSpec((1, sub * tq, MIX_WIDTH), tok_map),
            pl.BlockSpec((1, sub * tq, D_MODEL), tok_map),
            pl.BlockSpec((1, 1, D_MODEL), request),
            k_spec, vt_spec, k_spec, vt_spec,
            ctx_spec, ctx_spec, ctx_spec, ctx_spec,
            pl.BlockSpec(memory_space=pltpu.SMEM),
            pl.BlockSpec((MIX_WIDTH, D_MODEL), lambda i, j: (0, 0)),
        ],
        out_specs=pl.BlockSpec((1, sub * tq, D_MODEL), tok_map),
        out_shape=jax.ShapeDtypeStruct((b, n, D_MODEL), F32),
        scratch_shapes=[
            pltpu.VMEM((sub, MIX_WIDTH, tq), F32),
        ],
        compiler_params=pltpu.CompilerParams(
            dimension_semantics=("arbitrary", "arbitrary"), vmem_limit_bytes=VMEM_LIMIT_BYTES),
        name="attend_latent",
    )(q, g, x, gate, ka, vta, kb, vtb, *ctx, sink, w_out_bf)


def _ctx_kernel(*refs, tq, sub, exp_lead):
    (x_ref, shift_ref, scale_ref, gate_ref, gain_ref, w_ref, hg_ref, ones_ref, sink_ref, wout_ref,
     out_ref, ka32_ref, va32_ref, kb32_ref, vb32_ref,
     q_scr, g_scr, ka_scr, vta_scr, kb_scr, vtb_scr, o_scr) = refs
    _proj_kernel(x_ref, shift_ref, scale_ref, gain_ref, w_ref, hg_ref, ones_ref,
                 q_scr, g_scr, ka_scr, vta_scr, kb_scr, vtb_scr,
                 ka32_ref, va32_ref, kb32_ref, vb32_ref, rope=False, emit_f32=True)
    _attn_kernel(q_scr, g_scr, x_ref, gate_ref, ka_scr, vta_scr, kb_scr, vtb_scr,
                 sink_ref, wout_ref, out_ref, o_scr,
                 n_ctx=0, windowed=False, n_lat=tq, tq=tq, gb=GROUP, sub=sub, own_keys=True,
                 exp_lead=exp_lead, fill_ahead=CTX_FILL_AHEAD)


def _context_pass(x, shift, scale, gate, gain, w_in_bf, head_gains, ones_bd, sink, w_out_bf,
                  sub, exp_lead):
    requests, n, _ = x.shape
    tm = sub * n
    x_flat = x.reshape(1, requests * n, D_MODEL)
    const = lambda *idx: (lambda i: idx)
    tok_spec = pl.BlockSpec((1, tm, D_MODEL), lambda i: (0, i, 0))
    kv32_spec = pl.BlockSpec((sub, n, KV_W), lambda i: (i, 0, 0))
    outs = pl.pallas_call(
        functools.partial(_ctx_kernel, tq=n, sub=sub, exp_lead=exp_lead),
        grid=(requests // sub,),
        in_specs=[
            tok_spec,
            pl.BlockSpec((1, 1, D_MODEL), const(0, 0, 0)),
            pl.BlockSpec((1, 1, D_MODEL), const(0, 0, 0)),
            pl.BlockSpec((1, 1, D_MODEL), const(0, 0, 0)),
            pl.BlockSpec((1, D_MODEL), const(0, 0)),
            pl.BlockSpec((D_MODEL, IN_WIDTH), const(0, 0)),
            pl.BlockSpec((4, LANES), const(0, 0)),
            pl.BlockSpec((MXU_DIM, MXU_DIM), const(0, 0)),
            pl.BlockSpec(memory_space=pltpu.SMEM),
            pl.BlockSpec((MIX_WIDTH, D_MODEL), const(0, 0)),
        ],
        out_specs=[tok_spec] + [kv32_spec] * 4,
        out_shape=[jax.ShapeDtypeStruct((1, requests * n, D_MODEL), F32)]
        + [jax.ShapeDtypeStruct((requests, n, KV_W), F32)] * 4,
        scratch_shapes=[
            pltpu.VMEM((1, N_HEADS, tm, LANES), BF16),
            pltpu.VMEM((1, tm, MIX_WIDTH), BF16),
            pltpu.VMEM((1, tm, KV_W), BF16),
            pltpu.VMEM((1, KV_W, tm), BF16),
            pltpu.VMEM((1, tm, KV_W), BF16),
            pltpu.VMEM((1, KV_W, tm), BF16),
            pltpu.VMEM((sub, MIX_WIDTH, n), F32),
        ],
        compiler_params=pltpu.CompilerParams(
            dimension_semantics=("arbitrary",), vmem_limit_bytes=VMEM_LIMIT_BYTES),
        name="context_pass",
    )(x_flat, shift, scale, gate, gain, w_in_bf, head_gains, ones_bd, sink, w_out_bf)
    return (outs[0].reshape(requests, n, D_MODEL),) + tuple(outs[1:])


def _rope_tables(n_tokens):
    f32 = np.float32
    rows = n_tokens // GRID_W
    row = np.repeat(np.arange(rows, dtype=f32), GRID_W)
    col = np.tile(np.arange(GRID_W, dtype=f32), rows)
    n_freq = HEAD_DIM // 4
    inv = f32(ROPE_THETA) ** (-np.arange(n_freq, dtype=f32) / f32(n_freq))
    ar = row[:, None] * inv[None, :]
    ac = col[:, None] * inv[None, :]
    ang = np.concatenate([ar, ar, ac, ac], axis=-1).astype(f32)
    sign = np.tile(np.repeat(f32([-1.0, 1.0]), ROPE_QUARTER), 2)
    cos = np.tile(np.cos(ang), (1, LANES // HEAD_DIM))
    sin_signed = np.tile(np.sin(ang) * sign[None, :], (1, LANES // HEAD_DIM))
    return jnp.asarray(cos, F32), jnp.asarray(sin_signed, F32)


def _ones_blockdiag():
    idx = np.arange(MXU_DIM) // HEAD_DIM
    return jnp.asarray(idx[:, None] == idx[None, :], BF16)


def kernel(x_prompt, x_sample, cache_k_a, cache_v_a, cache_k_b, cache_v_b, c, c_ctx,
           w_mod, b_mod, norm_gain, w_in, qn_a, kn_a, qn_b, kn_b, sink_b, w_out):
    depth = w_in.shape[0]
    batch, seq, _ = x_prompt.shape
    dec_batch, dec_seq, _ = x_sample.shape
    past = cache_k_a.shape[2]

    rope_tables = _rope_tables(dec_seq)
    ones_bd = _ones_blockdiag()
    n_cond = 1 + dec_batch
    cond_rows = -(-n_cond // 8) * 8
    cond = jnp.concatenate(
        [c_ctx[None, :], c, jnp.zeros((cond_rows - n_cond, D_MODEL), F32)], axis=0)

    xp, xs = x_prompt, x_sample
    new_kv = [[], [], [], []]
    tile2 = lambda v: jnp.tile(v, LANES // HEAD_DIM)
    for l in range(depth):
        w_in_bf = w_in[l].astype(BF16)
        w_out_bf = w_out[l].astype(BF16)
        q_scale = HEAD_DIM ** -0.5 * LOG2E
        head_gains = jnp.stack([tile2(qn_a[l]) * q_scale, tile2(kn_a[l]),
                                tile2(qn_b[l]) * q_scale, tile2(kn_b[l])])
        gain = norm_gain[l].reshape(1, D_MODEL)
        sink = sink_b[l]

        m = _modulation(cond, w_mod[l], b_mod[l])
        shift, scale, gate = (m[:, i * D_MODEL:(i + 1) * D_MODEL] for i in range(3))

        sel = lambda v: v[0:1].reshape(1, 1, D_MODEL)
        xp, ka32, va32, kb32, vb32 = _context_pass(
            xp, sel(shift), sel(scale), sel(gate), gain, w_in_bf, head_gains, ones_bd, sink,
            w_out_bf, CTX_TILES_PER_STEP, CTX_EXP_LEAD)
        for acc, v in zip(new_kv, (ka32, va32, kb32, vb32)):
            acc.append(v.reshape(batch, seq, N_KV_A, HEAD_DIM))

        sel = lambda v: v[1:n_cond].reshape(dec_batch, 1, D_MODEL)
        q, g, ka, vta, kb, vtb = _project_latent(
            xs, sel(shift), sel(scale), gain, w_in_bf, head_gains, ones_bd, rope_tables)
        ctx = tuple(cache[:, l].reshape(dec_batch, past, KV_W)
                    for cache in (cache_k_a, cache_v_a, cache_k_b, cache_v_b))
        xs = _attend_latent(q, g, xs, sel(gate), ka, vta, kb, vtb, ctx, sink, w_out_bf,
                            LATENT_HEADS_PER_TASK, LATENT_TILES_PER_STEP, LATENT_EXP_LEAD)

    return (xp, xs) + tuple(jnp.stack(v, axis=1) for v in new_kv)
```
